```python
import math
import jax
import jax.numpy as jnp
from jax import lax
import numpy as np

D_MODEL = 1024
BATCH = 8
SEQ = 4096
DEPTH = 2
DEC_BATCH = 128
DEC_SEQ = 8
PAST_LEN = 16384
PAGE_SIZE = 128

HEAD_DIM = 64
MLA_HEADS = 8
MLA_Q_RANK = 256
MLA_KV_RANK = 256
MLA_NOPE_DIM = 64
MLA_ROPE_DIM = 32
MLA_V_DIM = 64
ROPE_THETA = 10000.0
FOX_HEADS = 8
FOX_KV_HEADS = 4
FOX_GROUP = FOX_HEADS // FOX_KV_HEADS
FORGET_BIAS_INIT = 2.0
MOBA_HEADS = 8
MOBA_KV_HEADS = 4
MOBA_GROUP = MOBA_HEADS // MOBA_KV_HEADS
MOBA_BLOCK = 256
MOBA_TOPK = 3
MOBA_Q_BLOCK = 64
GM_GROUPS = 4
GM_GROUP_DIM = 128
GM_WIDTH = GM_GROUPS * GM_GROUP_DIM
GM_CHUNK = 128
Q_BLOCK = 128
MOE_GROUPS = 4
MOE_PER_GROUP = 4
MOE_EXPERTS = MOE_GROUPS * MOE_PER_GROUP
MOE_TOPK = 2
MOE_FF = 256
EPS = 1e-6

MLA_QK_DIM = MLA_NOPE_DIM + MLA_ROPE_DIM
MLA_SCALE = MLA_QK_DIM ** -0.5
ATTN_SCALE = HEAD_DIM ** -0.5
COLS_A = [MLA_Q_RANK, MLA_KV_RANK, MLA_ROPE_DIM, FOX_HEADS * HEAD_DIM, FOX_KV_HEADS * HEAD_DIM, FOX_KV_HEADS * HEAD_DIM, FOX_HEADS]
IN_A = sum(COLS_A)
MIX_A = MLA_HEADS * MLA_V_DIM + FOX_HEADS * HEAD_DIM
COLS_B = [MOBA_HEADS * HEAD_DIM, MOBA_KV_HEADS * HEAD_DIM, MOBA_KV_HEADS * HEAD_DIM, 2 * GM_WIDTH]
IN_B = sum(COLS_B)
MIX_B = MOBA_HEADS * HEAD_DIM + GM_WIDTH

kernel_name = 'hybrid_mla_fox_moba_gmlp_hmoe_step'


def rms_norm(x, g):
    xf = x.astype(jnp.float32)
    y = xf * lax.rsqrt(jnp.mean(xf * xf, axis=-1, keepdims=True) + EPS)
    return (y * g.astype(jnp.float32)).astype(x.dtype)


def layer_norm(x, g, b):
    xf = x.astype(jnp.float32)
    xc = xf - jnp.mean(xf, axis=-1, keepdims=True)
    y = xc * lax.rsqrt(jnp.mean(xc * xc, axis=-1, keepdims=True) + EPS)
    return (y * g.astype(jnp.float32) + b.astype(jnp.float32)).astype(x.dtype)


def rotary(x, pos):
    half = MLA_ROPE_DIM // 2
    inv = jnp.exp(jnp.arange(half, dtype=jnp.float32) * (-2.0 * math.log(ROPE_THETA) / MLA_ROPE_DIM))
    ang = pos.astype(jnp.float32)[:, None] * inv[None, :]
    ang = ang.reshape((1, pos.shape[0]) + (1,) * (x.ndim - 3) + (half,))
    cos, sin = jnp.cos(ang), jnp.sin(ang)
    xf = x.astype(jnp.float32)
    x1, x2 = xf[..., :half], xf[..., half:]
    return jnp.concatenate([x1 * cos - x2 * sin, x2 * cos + x1 * sin], axis=-1).astype(x.dtype)


def split_cols(z, widths):
    out, start = [], 0
    for w in widths:
        out.append(z[..., start:start + w])
        start += w
    return out


def gather_pages(cache, layer, pt):
    rows = cache[layer, pt]
    return rows.reshape((-1,) + rows.shape[2:])


def pad_to_blocks(x, axis):
    pad = (-x.shape[axis]) % MOBA_BLOCK
    widths = [(0, 0)] * x.ndim
    widths[axis] = (0, pad)
    return jnp.pad(x, widths)


def softmax_attend(q, k, v, mask, scale, bias=None):
    s = jnp.einsum('bqhgd,blhd->bhgql', q, k).astype(jnp.float32) * scale
    if bias is not None:
        s = s + bias
    s = jnp.where(mask, s, -jnp.inf)
    p = jax.nn.softmax(s, axis=-1).astype(v.dtype)
    return jnp.einsum('bhgql,blhe->bqhge', p, v)


def causal_blocked(q, k, v, scale, c=None):
    B, T = q.shape[:2]
    nqb = T // Q_BLOCK
    kpos = jnp.arange(T)

    def blocks(a):
        return a.reshape((B, nqb, Q_BLOCK) + a.shape[2:]).swapaxes(0, 1)

    ck = None if c is None else jnp.moveaxis(c, 1, -1)[..., None, :]

    def step(args):
        j, q_j = args[0], args[1]
        qpos = j * Q_BLOCK + jnp.arange(Q_BLOCK)
        mask = kpos[None, :] <= qpos[:, None]
        bias = None if c is None else jnp.moveaxis(args[2], 1, -1)[..., None] - ck
        return softmax_attend(q_j, k, v, mask, scale, bias)

    xs = (jnp.arange(nqb), blocks(q)) + (() if c is None else (blocks(c),))
    o = lax.map(step, xs)
    return o.swapaxes(0, 1).reshape((B, T) + o.shape[3:])


def moba_attend(q, qpos, k, v):
    nb = k.shape[0] // MOBA_BLOCK
    kb = k.reshape(nb, MOBA_BLOCK, MOBA_KV_HEADS, HEAD_DIM).transpose(2, 0, 1, 3)
    vb = v.reshape(nb, MOBA_BLOCK, MOBA_KV_HEADS, HEAD_DIM).transpose(2, 0, 1, 3)
    kmean = jnp.mean(kb.astype(jnp.float32), axis=2)
    gate = jnp.einsum('qhgd,hnd->qhgn', q.astype(jnp.float32), kmean)
    cur = qpos // MOBA_BLOCK
    fully_past = jnp.arange(nb)[None, :] < cur[:, None]
    gate = jnp.where(fully_past[:, None, None, :], gate, -jnp.inf)
    top_val, top_idx = lax.top_k(gate, min(MOBA_TOPK, nb))
    own = jnp.broadcast_to(cur[:, None, None, None], top_idx.shape[:3] + (1,))
    blocks = jnp.concatenate([top_idx, own.astype(top_idx.dtype)], axis=-1)
    keep = jnp.concatenate([jnp.isfinite(top_val), jnp.ones(own.shape, dtype=bool)], axis=-1)
    hsel = jnp.arange(MOBA_KV_HEADS)[None, :, None, None]
    kg = kb[hsel, blocks]
    vg = vb[hsel, blocks]
    s = jnp.einsum('qhgd,qhgrkd->qhgrk', q, kg).astype(jnp.float32) * ATTN_SCALE
    kpos = blocks[..., None] * MOBA_BLOCK + jnp.arange(MOBA_BLOCK)
    valid = keep[..., None] & (kpos <= qpos[:, None, None, None, None])
    s = jnp.where(valid, s, -jnp.inf)
    Qn, Hk, G, R, K = s.shape
    p = jax.nn.softmax(s.reshape(Qn, Hk, G, R * K), axis=-1).reshape(s.shape).astype(v.dtype)
    return jnp.einsum('qhgrk,qhgrke->qhge', p, vg)


def mla_expand(c_kv, k_r, g_ckv, w_ukv, g_kn):
    kv = rms_norm(c_kv, g_ckv) @ w_ukv
    kv = kv.reshape(kv.shape[:-1] + (MLA_HEADS, MLA_NOPE_DIM + MLA_V_DIM))
    k_n = rms_norm(kv[..., :MLA_NOPE_DIM], g_kn)
    v = kv[..., MLA_NOPE_DIM:]
    k_rb = jnp.broadcast_to(k_r[..., None, :], k_n.shape[:-1] + (MLA_ROPE_DIM,))
    return jnp.concatenate([k_n, k_rb.astype(k_n.dtype)], axis=-1), v


def mla_fox_project(h, pos, w_in, g_cq, w_uq, g_qn, g_qr, g_kr, g_fq, g_fk, b_f):
    B, T, _ = h.shape
    c_q, c_kv, k_r, fq, fk, fv, fl = split_cols(h @ w_in, COLS_A)
    q = (rms_norm(c_q, g_cq) @ w_uq).reshape(B, T, MLA_HEADS, MLA_QK_DIM)
    q_m = jnp.concatenate([rms_norm(q[..., :MLA_NOPE_DIM], g_qn),
                           rotary(rms_norm(q[..., MLA_NOPE_DIM:], g_qr), pos)], axis=-1)
    k_r = rotary(rms_norm(k_r, g_kr), pos)
    q_f = rms_norm(fq.reshape(B, T, FOX_KV_HEADS, FOX_GROUP, HEAD_DIM), g_fq)
    k_f = rms_norm(fk.reshape(B, T, FOX_KV_HEADS, HEAD_DIM), g_fk)
    v_f = fv.reshape(B, T, FOX_KV_HEADS, HEAD_DIM)
    logf = jax.nn.log_sigmoid((fl + b_f).astype(jnp.float32))
    return q_m, c_kv, k_r, q_f, k_f, v_f, logf


def mla_fox_prompt(q_m, c_kv, k_r, q_f, k_f, v_f, logf, g_ckv, w_ukv, g_kn):
    B, T = q_m.shape[:2]
    k_m, v_m = mla_expand(c_kv, k_r, g_ckv, w_ukv, g_kn)
    o_m = causal_blocked(q_m[:, :, :, None], k_m, v_m, MLA_SCALE)
    c = jnp.cumsum(logf, axis=1).reshape(B, T, FOX_KV_HEADS, FOX_GROUP)
    o_f = causal_blocked(q_f, k_f, v_f, ATTN_SCALE, c)
    return jnp.concatenate([o_m.reshape(B, T, -1), o_f.reshape(B, T, -1)], axis=-1)


def mla_fox_sample(q_m, c_kv, k_r, q_f, k_f, v_f, logf, cache_ckv, cache_kr, cache_fk, cache_fv, cache_fl,
                   page_table, layer, g_ckv, w_ukv, g_kn):
    S = q_m.shape[1]
    qpos = PAST_LEN + jnp.arange(S)
    kpos = jnp.arange(PAST_LEN + S)
    mask = kpos[None, :] <= qpos[:, None]

    def per_seq(args):
        pt, qm, ckv_n, kr_n, qf, kf_n, vf_n, lf_n = args
        ckv = jnp.concatenate([gather_pages(cache_ckv, layer, pt), ckv_n], axis=0)
        kr = jnp.concatenate([gather_pages(cache_kr, layer, pt), kr_n], axis=0)
        k_m, v_m = mla_expand(ckv, kr, g_ckv, w_ukv, g_kn)
        o_m = softmax_attend(qm[None, :, :, None], k_m[None], v_m[None], mask, MLA_SCALE)[0]
        kf = jnp.concatenate([gather_pages(cache_fk, layer, pt), kf_n], axis=0)
        vf = jnp.concatenate([gather_pages(cache_fv, layer, pt), vf_n], axis=0)
        lf = jnp.concatenate([gather_pages(cache_fl, layer, pt).astype(jnp.float32), lf_n], axis=0)
        ct = jnp.moveaxis(jnp.cumsum(lf, axis=0).reshape(-1, FOX_KV_HEADS, FOX_GROUP), 0, -1)
        bias = ct[:, :, PAST_LEN:, None] - ct[:, :, None, :]
        o_f = softmax_attend(qf[None], kf[None], vf[None], mask, ATTN_SCALE, bias[None])[0]
        return jnp.concatenate([o_m.reshape(S, -1), o_f.reshape(S, -1)], axis=-1)

    return lax.map(per_seq, (page_table, q_m, c_kv, k_r, q_f, k_f, v_f, logf))


def moba_gmlp_project(h, w_in, g_mq, g_mk, ln_g, ln_b):
    B, T, _ = h.shape
    mq, mk, mv, z = split_cols(h @ w_in, COLS_B)
    q = rms_norm(mq.reshape(B, T, MOBA_KV_HEADS, MOBA_GROUP, HEAD_DIM), g_mq)
    k = rms_norm(mk.reshape(B, T, MOBA_KV_HEADS, HEAD_DIM), g_mk)
    v = mv.reshape(B, T, MOBA_KV_HEADS, HEAD_DIM)
    z = jax.nn.gelu(z)
    u, gv = z[..., :GM_WIDTH], layer_norm(z[..., GM_WIDTH:], ln_g, ln_b)
    return q, k, v, u, gv


def moba_prompt(q, k, v):
    B, T = q.shape[:2]
    nqb = T // MOBA_Q_BLOCK
    kp, vp = pad_to_blocks(k, 1), pad_to_blocks(v, 1)
    qb = q.reshape((B * nqb, MOBA_Q_BLOCK) + q.shape[2:])
    b_idx = jnp.repeat(jnp.arange(B), nqb)
    q0 = jnp.tile(jnp.arange(nqb) * MOBA_Q_BLOCK, B)

    def step(args):
        b, s0, q_j = args
        return moba_attend(q_j, s0 + jnp.arange(MOBA_Q_BLOCK), kp[b], vp[b])

    o = lax.map(step, (b_idx, q0, qb))
    return o.reshape(B, T, -1)


def moba_sample(q, k, v, cache_mk, cache_mv, page_table, layer):
    S = q.shape[1]
    qpos = PAST_LEN + jnp.arange(S)

    def per_seq(args):
        pt, q_s, k_n, v_n = args
        k_all = pad_to_blocks(jnp.concatenate([gather_pages(cache_mk, layer, pt), k_n], axis=0), 0)
        v_all = pad_to_blocks(jnp.concatenate([gather_pages(cache_mv, layer, pt), v_n], axis=0), 0)
        return moba_attend(q_s, qpos, k_all, v_all).reshape(S, -1)

    return lax.map(per_seq, (page_table, q, k, v))


def gmlp_prompt(u, gv, w_s, b_s):
    B, T, _ = u.shape
    tri = jnp.tril(jnp.ones((GM_CHUNK, GM_CHUNK), dtype=bool))
    w = jnp.where(tri[None], w_s, 0.0)
    vg = gv.reshape(B, T // GM_CHUNK, GM_CHUNK, GM_GROUPS, GM_GROUP_DIM)
    mix = jnp.einsum('gts,bnsgd->bntgd', w, vg) + b_s.T[None, None, :, :, None]
    return u * mix.reshape(B, T, GM_WIDTH).astype(u.dtype)


def gmlp_sample(u, gv, w_s, b_s):
    Bd, S, _ = u.shape
    pos = PAST_LEN + jnp.arange(S)
    cp, ch = pos % GM_CHUNK, pos // GM_CHUNK
    mask = (ch[:, None] == ch[None, :]) & (pos[None, :] <= pos[:, None])
    w = jnp.where(mask[None], w_s[:, cp[:, None], cp[None, :]], 0.0)
    vg = gv.reshape(Bd, S, GM_GROUPS, GM_GROUP_DIM)
    mix = jnp.einsum('gts,bsgd->btgd', w, vg) + b_s[:, cp].T[None, :, :, None]
    return u * mix.reshape(Bd, S, GM_WIDTH).astype(u.dtype)


def hier_moe(x, w_group, b_group, w_sub, b_sub, w1, w3, w2):
    shp = x.shape
    xt = x.reshape(-1, shp[-1])
    g_logits = (xt @ w_group + b_group).astype(jnp.float32)
    grp = jnp.argmax(g_logits, axis=-1)
    p_grp = jnp.take_along_axis(jax.nn.softmax(g_logits, axis=-1), grp[:, None], axis=-1)
    sub_logits = (jnp.einsum('nd,gde->nge', xt, w_sub) + b_sub).astype(jnp.float32)
    sub = jnp.take_along_axis(sub_logits, grp[:, None, None], axis=1)[:, 0]
    top_v, top_i = lax.top_k(sub, MOE_TOPK)
    w_top = jax.nn.softmax(top_v, axis=-1) * p_grp
    eidx = grp[:, None] * MOE_PER_GROUP + top_i
    gate = jnp.sum(jax.nn.one_hot(eidx, MOE_EXPERTS, dtype=jnp.float32) * w_top[..., None], axis=1)
    y = jnp.zeros_like(xt)
    for e in range(MOE_EXPERTS):
        h = jax.nn.silu(xt @ w1[e]) * (xt @ w3[e])
        y = y + gate[:, e:e + 1].astype(xt.dtype) * (h @ w2[e])
    return y.reshape(shp)


def setup_inputs(seed: int = 0) -> dict:
    key = jax.random.key(seed)
    keys = list(jax.random.split(key, 64))

    def nrm(shape, scale=1.0):
        return scale * jax.random.normal(keys.pop(), shape, jnp.float32)

    def gain(shape):
        return 1.0 + 0.1 * jax.random.normal(keys.pop(), shape, jnp.float32)

    n_even, n_odd = (DEPTH + 1) // 2, DEPTH // 2
    n_pages = PAST_LEN // PAGE_SIZE
    n_used = DEC_BATCH * n_pages
    n_phys = n_used + max(1, n_used // 4)
    page_table = jax.random.permutation(keys.pop(), n_phys)[:n_used].reshape(DEC_BATCH, n_pages).astype(jnp.int32)
    pool = (n_phys, PAGE_SIZE)
    return {
        'x_prompt': nrm((BATCH, SEQ, D_MODEL)),
        'x_sample': nrm((DEC_BATCH, DEC_SEQ, D_MODEL)),
        'cache_mla_ckv': nrm((n_even,) + pool + (MLA_KV_RANK,)),
        'cache_mla_krope': nrm((n_even,) + pool + (MLA_ROPE_DIM,)),
        'cache_fox_k': nrm((n_even,) + pool + (FOX_KV_HEADS, HEAD_DIM)),
        'cache_fox_v': nrm((n_even,) + pool + (FOX_KV_HEADS, HEAD_DIM)),
        'cache_fox_logf': jax.nn.log_sigmoid(FORGET_BIAS_INIT + nrm((n_even,) + pool + (FOX_HEADS,))),
        'cache_moba_k': nrm((n_odd,) + pool + (MOBA_KV_HEADS, HEAD_DIM)),
        'cache_moba_v': nrm((n_odd,) + pool + (MOBA_KV_HEADS, HEAD_DIM)),
        'page_table': page_table,
        'norm_mix': gain((DEPTH, D_MODEL)),
        'norm_ffn': gain((DEPTH, D_MODEL)),
        'w_in_a': nrm((n_even, D_MODEL, IN_A), D_MODEL ** -0.5),
        'mla_g_cq': gain((n_even, MLA_Q_RANK)),
        'mla_w_uq': nrm((n_even, MLA_Q_RANK, MLA_HEADS * MLA_QK_DIM), MLA_Q_RANK ** -0.5),
        'mla_g_ckv': gain((n_even, MLA_KV_RANK)),
        'mla_w_ukv': nrm((n_even, MLA_KV_RANK, MLA_HEADS * (MLA_NOPE_DIM + MLA_V_DIM)), MLA_KV_RANK ** -0.5),
        'mla_g_qn': gain((n_even, MLA_NOPE_DIM)),
        'mla_g_qr': gain((n_even, MLA_ROPE_DIM)),
        'mla_g_kn': gain((n_even, MLA_NOPE_DIM)),
        'mla_g_kr': gain((n_even, MLA_ROPE_DIM)),
        'fox_g_q': gain((n_even, HEAD_DIM)),
        'fox_g_k': gain((n_even, HEAD_DIM)),
        'fox_b_f': FORGET_BIAS_INIT + nrm((n_even, FOX_HEADS), 0.5),
        'w_out_a': nrm((n_even, MIX_A, D_MODEL), MIX_A ** -0.5),
        'w_in_b': nrm((n_odd, D_MODEL, IN_B), D_MODEL ** -0.5),
        'moba_g_q': gain((n_odd, HEAD_DIM)),
        'moba_g_k': gain((n_odd, HEAD_DIM)),
        'gm_ln_g': gain((n_odd, GM_WIDTH)),
        'gm_ln_b': nrm((n_odd, GM_WIDTH), 0.02),
        'gm_w_s': nrm((n_odd, GM_GROUPS, GM_CHUNK, GM_CHUNK), GM_CHUNK ** -0.5),
        'gm_b_s': gain((n_odd, GM_GROUPS, GM_CHUNK)),
        'w_out_b': nrm((n_odd, MIX_B, D_MODEL), MIX_B ** -0.5),
        'moe_w_group': nrm((DEPTH, D_MODEL, MOE_GROUPS), D_MODEL ** -0.5),
        'moe_b_group': nrm((DEPTH, MOE_GROUPS), 0.01),
        'moe_w_sub': nrm((DEPTH, MOE_GROUPS, D_MODEL, MOE_PER_GROUP), D_MODEL ** -0.5),
        'moe_b_sub': nrm((DEPTH, MOE_GROUPS, MOE_PER_GROUP), 0.01),
        'moe_w1': nrm((DEPTH, MOE_EXPERTS, D_MODEL, MOE_FF), D_MODEL ** -0.5),
        'moe_w3': nrm((DEPTH, MOE_EXPERTS, D_MODEL, MOE_FF), D_MODEL ** -0.5),
        'moe_w2': nrm((DEPTH, MOE_EXPERTS, MOE_FF, D_MODEL), MOE_FF ** -0.5),
    }


def reference(x_prompt, x_sample, cache_mla_ckv, cache_mla_krope, cache_fox_k, cache_fox_v, cache_fox_logf,
              cache_moba_k, cache_moba_v, page_table, norm_mix, norm_ffn, w_in_a, mla_g_cq, mla_w_uq, mla_g_ckv,
              mla_w_ukv, mla_g_qn, mla_g_qr, mla_g_kn, mla_g_kr, fox_g_q, fox_g_k, fox_b_f, w_out_a, w_in_b,
              moba_g_q, moba_g_k, gm_ln_g, gm_ln_b, gm_w_s, gm_b_s, w_out_b, moe_w_group, moe_b_group,
              moe_w_sub, moe_b_sub, moe_w1, moe_w3, moe_w2):
    xp, xs = x_prompt, x_sample
    pos_p = jnp.arange(xp.shape[1])
    pos_s = PAST_LEN + jnp.arange(xs.shape[1])
    p_ckv, p_kr, p_fk, p_fv, p_fl, p_mk, p_mv = [], [], [], [], [], [], []
    s_ckv, s_kr, s_fk, s_fv, s_fl, s_mk, s_mv, s_gv = [], [], [], [], [], [], [], []
    for layer in range(DEPTH):
        hp = rms_norm(xp, norm_mix[layer])
        hs = rms_norm(xs, norm_mix[layer])
        i = layer // 2
        if layer % 2 == 0:
            proj_w = (w_in_a[i], mla_g_cq[i], mla_w_uq[i], mla_g_qn[i], mla_g_qr[i], mla_g_kr[i],
                      fox_g_q[i], fox_g_k[i], fox_b_f[i])
            ap = mla_fox_project(hp, pos_p, *proj_w)
            asm = mla_fox_project(hs, pos_s, *proj_w)
            mix_p = mla_fox_prompt(*ap, mla_g_ckv[i], mla_w_ukv[i], mla_g_kn[i])
            mix_s = mla_fox_sample(*asm, cache_mla_ckv, cache_mla_krope, cache_fox_k, cache_fox_v, cache_fox_logf,
                                   page_table, i, mla_g_ckv[i], mla_w_ukv[i], mla_g_kn[i])
            xp = xp + mix_p @ w_out_a[i]
            xs = xs + mix_s @ w_out_a[i]
            p_ckv.append(ap[1]); p_kr.append(ap[2]); p_fk.append(ap[4]); p_fv.append(ap[5]); p_fl.append(ap[6])
            s_ckv.append(asm[1]); s_kr.append(asm[2]); s_fk.append(asm[4]); s_fv.append(asm[5]); s_fl.append(asm[6])
        else:
            proj_w = (w_in_b[i], moba_g_q[i], moba_g_k[i], gm_ln_g[i], gm_ln_b[i])
            q_p, k_p, v_p, u_p, gv_p = moba_gmlp_project(hp, *proj_w)
            q_s, k_s, v_s, u_s, gv_s = moba_gmlp_project(hs, *proj_w)
            mix_p = jnp.concatenate([moba_prompt(q_p, k_p, v_p), gmlp_prompt(u_p, gv_p, gm_w_s[i], gm_b_s[i])], axis=-1)
            mix_s = jnp.concatenate([moba_sample(q_s, k_s, v_s, cache_moba_k, cache_moba_v, page_table, i),
                                     gmlp_sample(u_s, gv_s, gm_w_s[i], gm_b_s[i])], axis=-1)
            xp = xp + mix_p @ w_out_b[i]
            xs = xs + mix_s @ w_out_b[i]
            p_mk.append(k_p); p_mv.append(v_p)
            s_mk.append(k_s); s_mv.append(v_s); s_gv.append(gv_s)
        moe_w = (moe_w_group[layer], moe_b_group[layer], moe_w_sub[layer], moe_b_sub[layer],
                 moe_w1[layer], moe_w3[layer], moe_w2[layer])
        xp = xp + hier_moe(rms_norm(xp, norm_ffn[layer]), *moe_w)
        xs = xs + hier_moe(rms_norm(xs, norm_ffn[layer]), *moe_w)
    y_prompt, y_sample = xp, xs
    return (y_prompt, y_sample,
            jnp.stack(p_ckv), jnp.stack(p_kr), jnp.stack(p_fk), jnp.stack(p_fv), jnp.stack(p_fl),
            jnp.stack(p_mk), jnp.stack(p_mv),
            jnp.stack(s_ckv), jnp.stack(s_kr), jnp.stack(s_fk), jnp.stack(s_fv), jnp.stack(s_fl),
            jnp.stack(s_mk), jnp.stack(s_mv), jnp.stack(s_gv))
```

```python
import functools
import math

import jax
import jax.numpy as jnp
import numpy as np
from jax import lax
from jax.experimental import pallas as pl
from jax.experimental.pallas import tpu as pltpu

F32 = jnp.float32
BF16 = jnp.bfloat16

D_MODEL = 1024
HEAD_DIM = 64
MLA_HEADS = 8
MLA_Q_RANK = 256
MLA_KV_RANK = 256
MLA_NOPE_DIM = 64
MLA_ROPE_DIM = 32
MLA_V_DIM = 64
ROPE_THETA = 10000.0
FOX_HEADS = 8
FOX_KV_HEADS = 4
MOBA_HEADS = 8
MOBA_KV_HEADS = 4
MOBA_BLOCK = 256
MOBA_TOPK = 3
GM_GROUPS = 4
GM_GROUP_DIM = 128
GM_WIDTH = GM_GROUPS * GM_GROUP_DIM
GM_CHUNK = 128
MOE_GROUPS = 4
MOE_PER_GROUP = 4
MOE_EXPERTS = 16
MOE_FF = 256
EPS = 1e-6
MLA_SCALE = (MLA_NOPE_DIM + MLA_ROPE_DIM) ** -0.5
ATTN_SCALE = HEAD_DIM ** -0.5

LANES = 128
ROW_TILE = 256
MOBA_MAX_BLOCKS = 16
NEG_BIG = -(2.0 ** 100)
M_INIT = -1e30
VMEM_LIMIT = 48 * 1024 * 1024


def _dot(a, b):
    return jnp.dot(a, b, preferred_element_type=F32)


def _dot_nt(a, b):
    return lax.dot_general(a, b, (((1,), (1,)), ((), ())), preferred_element_type=F32)


def _split3(x):
    hi = x.astype(BF16)
    r = x - hi.astype(F32)
    mid = r.astype(BF16)
    lo = (r - mid.astype(F32)).astype(BF16)
    return hi, mid, lo


def _lane_iota(shape):
    return lax.broadcasted_iota(jnp.int32, shape, len(shape) - 1)


def _head_block(x, h, width):
    start = h * width
    blk = x[:, (start // LANES) * LANES:(start // LANES + 1) * LANES]
    off = start % LANES
    if off:
        blk = pltpu.roll(blk, LANES - off, axis=1)
    return jnp.where(_lane_iota(blk.shape) < width, blk, 0.0)


def _group_sumsq(x, gmat):
    outs = []
    for j in range(x.shape[1] // 256):
        xs = x[:, 256 * j:256 * (j + 1)]
        outs.append(_dot((xs * xs).astype(BF16), gmat))
    return outs[0] if len(outs) == 1 else jnp.concatenate(outs, axis=1)


def _rms_rows(x, g):
    return x * lax.rsqrt(jnp.mean(x * x, axis=-1, keepdims=True) + EPS) * g


def _rotate(x, rot_ref):
    c, s1, s2 = rot_ref[0], rot_ref[1], rot_ref[2]
    return x * c + pltpu.roll(x, 16, axis=1) * s1 + pltpu.roll(x, LANES - 16, axis=1) * s2


def _proj_a_kernel(x_ref, vec_ref, win_ref, wuq_ref, wuk_ref, wuv_ref, gmq_ref, gmc_ref, rot_ref,
                   tri_ref, pcum_ref,
                   ckv_o, kr_o, fk_o, fv_o, fl_o, qm_o, km_o, vm_o, qf_o, kf_o, vf_o,
                   carry_sc, *, tiles_per_seq):
    i = pl.program_id(0)
    x = x_ref[...]
    h = _rms_rows(x, vec_ref[0:1, :]).astype(BF16)
    z = _dot(h, win_ref[...])

    cqn = _rms_rows(z[:, 0:256], vec_ref[1:2, 0:256]).astype(BF16)
    c_kv = z[:, 256:512]
    ckv_o[...] = c_kv
    ckvn = _rms_rows(c_kv, vec_ref[1:2, 256:512]).astype(BF16)

    krb = z[:, 512:640]
    krb = krb * lax.rsqrt(jnp.sum(krb * krb, axis=-1, keepdims=True) * (1.0 / MLA_ROPE_DIM) + EPS)
    krb = _rotate(krb * vec_ref[1:2, 512:640], rot_ref)
    kr_o[...] = krb[:, 64:96]

    gmq = gmq_ref[...]
    q = _dot(cqn, wuq_ref[...])
    q = q * lax.rsqrt(_group_sumsq(q, gmq) * vec_ref[3:4, :] + EPS) * vec_ref[2:3, :]
    kx = _dot(ckvn, wuk_ref[...])
    kx = kx * lax.rsqrt(_group_sumsq(kx, gmq) * vec_ref[3:4, :] + EPS) * vec_ref[4:5, :]
    vx = _dot(ckvn, wuv_ref[...])
    for hh in range(MLA_HEADS):
        qb = q[:, LANES * hh:LANES * (hh + 1)]
        qm_o[hh] = _rotate(qb, rot_ref).astype(BF16)
        km_o[hh] = (kx[:, LANES * hh:LANES * (hh + 1)] + krb).astype(BF16)
        vm_o[hh] = vx[:, 64 * hh:64 * (hh + 1)].astype(BF16)

    gmc = gmc_ref[...]
    fq = z[:, 640:1152]
    fq = fq * lax.rsqrt(_group_sumsq(fq, gmc) * (1.0 / HEAD_DIM) + EPS) * vec_ref[5:6, 0:512]
    fk = z[:, 1152:1408]
    fk = fk * lax.rsqrt(_group_sumsq(fk, gmc) * (1.0 / HEAD_DIM) + EPS) * vec_ref[5:6, 512:768]
    fk_o[...] = fk
    fv = z[:, 1408:1664]
    fv_o[...] = fv
    f = z[:, 1664:1792] + vec_ref[1:2, 640:768]
    logf = jnp.minimum(f, 0.0) - jnp.log1p(jnp.exp(-jnp.abs(f)))
    fl_o[...] = logf[:, 0:FOX_HEADS]

    @pl.when(i % tiles_per_seq == 0)
    def _():
        carry_sc[...] = jnp.zeros_like(carry_sc)

    tri = tri_ref[...]
    lh, lm, ll = _split3(logf)
    cum = _dot(tri, lh) + _dot(tri, lm) + _dot(tri, ll) + carry_sc[...]
    carry_sc[...] = cum[ROW_TILE - 1:ROW_TILE, :]
    ch, cm, cl = _split3(cum)
    aug = _dot(ch, pcum_ref[0]) + _dot(cm, pcum_ref[1]) + _dot(cl, pcum_ref[2])

    for hh in range(FOX_HEADS):
        g = hh % 2
        qf_o[hh] = (_head_block(fq, hh, HEAD_DIM) + vec_ref[6 + g:7 + g, 0:LANES]).astype(BF16)
    for hk in range(FOX_KV_HEADS):
        kf_o[hk] = (_head_block(fk, hk, HEAD_DIM) + aug[:, LANES * hk:LANES * (hk + 1)]).astype(BF16)
        vf_o[hk] = fv[:, 64 * hk:64 * (hk + 1)].astype(BF16)


def _rot_tables(pos):
    half = MLA_ROPE_DIM // 2
    inv = jnp.exp(jnp.arange(half, dtype=F32) * (-2.0 * math.log(ROPE_THETA) / MLA_ROPE_DIM))
    ang = pos.astype(F32)[:, None] * inv[None, :]
    cos, sin = jnp.cos(ang), jnp.sin(ang)
    n = pos.shape[0]
    c = jnp.ones((n, LANES), F32).at[:, 64:80].set(cos).at[:, 80:96].set(cos)
    s1 = jnp.zeros((n, LANES), F32).at[:, 80:96].set(sin)
    s2 = jnp.zeros((n, LANES), F32).at[:, 64:80].set(-sin)
    return jnp.stack([c, s1, s2])


def _prep_a(i, norm_mix_l, w_in_a, mla_g_cq, mla_w_uq, mla_g_ckv, mla_w_ukv, mla_g_qn, mla_g_qr, mla_g_kn,
            mla_g_kr, fox_g_q, fox_g_k, fox_b_f):
    w = w_in_a[i]
    z128 = jnp.zeros((D_MODEL, LANES), F32)
    w_kr = z128.at[:, 64:96].set(w[:, 512:544])
    w_fl = z128.at[:, 0:FOX_HEADS].set(w[:, 1568:1576])
    win = jnp.concatenate([w[:, 0:512], w_kr, w[:, 544:1568], w_fl], axis=1).astype(BF16)

    wuq = mla_w_uq[i].reshape(MLA_Q_RANK, MLA_HEADS, 96)
    wuq = jnp.pad(wuq, ((0, 0), (0, 0), (0, 32))).reshape(MLA_Q_RANK, MLA_HEADS * LANES).astype(BF16)
    wukv = mla_w_ukv[i].reshape(MLA_KV_RANK, MLA_HEADS, 128)
    wuk = jnp.pad(wukv[:, :, 0:64], ((0, 0), (0, 0), (0, 64))).reshape(MLA_KV_RANK, MLA_HEADS * LANES).astype(BF16)
    wuv = wukv[:, :, 64:128].reshape(MLA_KV_RANK, MLA_HEADS * 64).astype(BF16)

    lane = np.arange(256) % LANES
    grp = np.where(lane < 64, 0, np.where(lane < 96, 1, -1)) + 2 * (np.arange(256) // LANES)
    valid = (lane < 96)
    gmq = ((grp[:, None] == grp[None, :]) & valid[:, None] & valid[None, :]).astype(np.float32)
    g64 = np.arange(256) // 64
    gmc = (g64[:, None] == g64[None, :]).astype(np.float32)

    zero32 = jnp.zeros((32,), F32)
    zero64 = jnp.zeros((64,), F32)
    qgain = jnp.tile(jnp.concatenate([mla_g_qn[i], mla_g_qr[i], zero32]) * MLA_SCALE, MLA_HEADS)
    invw = jnp.tile(jnp.concatenate([jnp.full((64,), 1 / 64., F32), jnp.full((32,), 1 / 32., F32), zero32]),
                    MLA_HEADS)
    kgain = jnp.tile(jnp.concatenate([mla_g_kn[i], zero64]), MLA_HEADS)
    g_kr_blk = jnp.zeros((LANES,), F32).at[64:96].set(mla_g_kr[i])
    b_f_blk = jnp.zeros((LANES,), F32).at[0:FOX_HEADS].set(fox_b_f[i])
    row1 = jnp.concatenate([mla_g_cq[i], mla_g_ckv[i], g_kr_blk, b_f_blk, jnp.zeros((256,), F32)])
    row5 = jnp.concatenate([jnp.tile(fox_g_q[i], FOX_HEADS) * ATTN_SCALE, jnp.tile(fox_g_k[i], FOX_KV_HEADS),
                            jnp.zeros((256,), F32)])
    qa0 = jnp.zeros((D_MODEL,), F32).at[64:67].set(1.0)
    qa1 = jnp.zeros((D_MODEL,), F32).at[67:70].set(1.0)
    vec = jnp.stack([norm_mix_l, row1, qgain, invw, kgain, row5, qa0, qa1])

    tri = np.tril(np.ones((ROW_TILE, ROW_TILE), np.float32))
    pcum = np.zeros((3, LANES, FOX_KV_HEADS * LANES), np.float32)
    for hk in range(FOX_KV_HEADS):
        for g in range(2):
            for j in range(3):
                pcum[j, 2 * hk + g, LANES * hk + 64 + 3 * g + j] = -1.0
    return dict(vec=vec, win=win, wuq=wuq, wuk=wuk, wuv=wuv, gmq=jnp.asarray(gmq, BF16),
                gmc=jnp.asarray(gmc, BF16), tri=jnp.asarray(tri, BF16), pcum=jnp.asarray(pcum, BF16))


def _const_spec(a):
    nd = a.ndim
    return pl.BlockSpec(a.shape, lambda *_: (0,) * nd)


def _proj_a(x2d, prm, rot, tiles_per_seq):
    n = x2d.shape[0]
    nt = n // ROW_TILE
    rt = rot.shape[1] // ROW_TILE
    row = lambda w: pl.BlockSpec((ROW_TILE, w), lambda i: (i, 0))
    hm = lambda nh, w: pl.BlockSpec((nh, ROW_TILE, w), lambda i: (0, i, 0))
    consts = [prm[k] for k in ("vec", "win", "wuq", "wuk", "wuv", "gmq", "gmc")]
    in_specs = ([row(D_MODEL)] + [_const_spec(a) for a in consts]
                + [pl.BlockSpec((3, ROW_TILE, LANES), lambda i: (0, i % rt, 0)),
                   _const_spec(prm["tri"]), _const_spec(prm["pcum"])])
    out_shape = [jax.ShapeDtypeStruct((n, 256), F32), jax.ShapeDtypeStruct((n, MLA_ROPE_DIM), F32),
                 jax.ShapeDtypeStruct((n, 256), F32), jax.ShapeDtypeStruct((n, 256), F32),
                 jax.ShapeDtypeStruct((n, FOX_HEADS), F32),
                 jax.ShapeDtypeStruct((MLA_HEADS, n, LANES), BF16), jax.ShapeDtypeStruct((MLA_HEADS, n, LANES), BF16),
                 jax.ShapeDtypeStruct((MLA_HEADS, n, 64), BF16),
                 jax.ShapeDtypeStruct((FOX_HEADS, n, LANES), BF16), jax.ShapeDtypeStruct((FOX_KV_HEADS, n, LANES), BF16),
                 jax.ShapeDtypeStruct((FOX_KV_HEADS, n, 64), BF16)]
    out_specs = [row(256), row(MLA_ROPE_DIM), row(256), row(256), row(FOX_HEADS),
                 hm(MLA_HEADS, LANES), hm(MLA_HEADS, LANES), hm(MLA_HEADS, 64),
                 hm(FOX_HEADS, LANES), hm(FOX_KV_HEADS, LANES), hm(FOX_KV_HEADS, 64)]
    return pl.pallas_call(
        functools.partial(_proj_a_kernel, tiles_per_seq=tiles_per_seq),
        grid=(nt,), in_specs=in_specs, out_specs=out_specs, out_shape=out_shape,
        scratch_shapes=[pltpu.VMEM((1, LANES), F32)],
        compiler_params=pltpu.CompilerParams(dimension_semantics=("arbitrary",), vmem_limit_bytes=VMEM_LIMIT),
        name="proj_a",
    )(x2d, *consts, rot, prm["tri"], prm["pcum"])


def _flash_kernel(q_ref, k_ref, v_ref, o_ref, m_sc, l_sc, acc_sc, *, tq, tk, kv_shared):
    qi = pl.program_id(2)
    ki = pl.program_id(3)
    last_k = (qi * tq + tq - 1) // tk

    @pl.when(ki == 0)
    def _():
        m_sc[...] = jnp.full_like(m_sc, M_INIT)
        l_sc[...] = jnp.zeros_like(l_sc)
        acc_sc[...] = jnp.zeros_like(acc_sc)

    def step(masked):
        for hh in range(2):
            kvh = 0 if kv_shared else hh
            s = _dot_nt(q_ref[hh], k_ref[kvh])
            if masked:
                rows = qi * tq + lax.broadcasted_iota(jnp.int32, s.shape, 0)
                cols = ki * tk + lax.broadcasted_iota(jnp.int32, s.shape, 1)
                s = jnp.where(cols <= rows, s, -jnp.inf)
            m_old = m_sc[hh]
            m_new = jnp.maximum(m_old, jnp.max(s, axis=-1, keepdims=True))
            alpha = jnp.exp(m_old - m_new)
            p = jnp.exp(s - m_new)
            l_sc[hh] = alpha * l_sc[hh] + jnp.sum(p, axis=-1, keepdims=True)
            acc_sc[hh] = alpha * acc_sc[hh] + _dot(p.astype(BF16), v_ref[kvh])
            m_sc[hh] = m_new

    needs_mask = ki * tk + tk - 1 > qi * tq

    @pl.when(jnp.logical_and(ki <= last_k, needs_mask))
    def _():
        step(True)

    @pl.when(jnp.logical_and(ki <= last_k, jnp.logical_not(needs_mask)))
    def _():
        step(False)

    @pl.when(ki == last_k)
    def _():
        for hh in range(2):
            o_ref[:, 64 * hh:64 * (hh + 1)] = (acc_sc[hh] / l_sc[hh]).astype(o_ref.dtype)


def _flash(q, k, v, batch, seq, *, tq, tk):
    hq, n, _ = q.shape
    hkv = k.shape[0]
    kv_shared = hkv * 2 == hq
    kvb = 1 if kv_shared else 2
    nq, nk = seq // tq, seq // tk

    def kv_map(b, hp, qi, ki):
        return (hp, b * nk + jnp.minimum(ki, (qi * tq + tq - 1) // tk), 0)

    return pl.pallas_call(
        functools.partial(_flash_kernel, tq=tq, tk=tk, kv_shared=kv_shared),
        grid=(batch, hq // 2, nq, nk),
        in_specs=[pl.BlockSpec((2, tq, LANES), lambda b, hp, qi, ki: (hp, b * nq + qi, 0)),
                  pl.BlockSpec((kvb, tk, LANES), kv_map),
                  pl.BlockSpec((kvb, tk, 64), kv_map)],
        out_specs=pl.BlockSpec((tq, LANES), lambda b, hp, qi, ki: (b * nq + qi, hp)),
        out_shape=jax.ShapeDtypeStruct((n, hq * 64), BF16),
        scratch_shapes=[pltpu.VMEM((2, tq, 1), F32), pltpu.VMEM((2, tq, 1), F32), pltpu.VMEM((2, tq, 64), F32)],
        compiler_params=pltpu.CompilerParams(
            dimension_semantics=("parallel", "parallel", "parallel", "arbitrary"), vmem_limit_bytes=VMEM_LIMIT),
        name="flash",
    )(q, k, v)


def _first_index_of_max(vals, lane):
    vmax = jnp.max(vals, axis=-1, keepdims=True)
    idx = jnp.min(jnp.where(vals == vmax, lane, 4 * LANES), axis=-1, keepdims=True)
    return vmax, idx


def _route(hm, wr_ref, br_ref):
    hh, hl = hm.astype(BF16), None
    hl = (hm - hh.astype(F32)).astype(BF16)
    logits = _dot(hh, wr_ref[0]) + _dot(hh, wr_ref[1]) + _dot(hl, wr_ref[0]) + br_ref[...]
    lane = _lane_iota(logits.shape)
    gl = jnp.where(lane < MOE_GROUPS, logits, -jnp.inf)
    gmax, grp = _first_index_of_max(gl, lane)
    p_grp = 1.0 / jnp.sum(jnp.exp(gl - gmax), axis=-1, keepdims=True)
    lo = MOE_GROUPS + MOE_PER_GROUP * grp
    sl = jnp.where(jnp.logical_and(lane >= lo, lane < lo + MOE_PER_GROUP), logits, -jnp.inf)
    v1, i1 = _first_index_of_max(sl, lane)
    sl2 = jnp.where(lane == i1, -jnp.inf, sl)
    v2, i2 = _first_index_of_max(sl2, lane)
    e2 = jnp.exp(v2 - v1)
    w1 = p_grp / (1.0 + e2)
    w2 = p_grp * e2 / (1.0 + e2)
    return jnp.where(lane == i1, w1, 0.0) + jnp.where(lane == i2, w2, 0.0)


def _out_moe_kernel(x_ref, mix_ref, wout_ref, gffn_ref, wr_ref, br_ref, w1_ref, w3_ref, w2_ref, o_ref,
                    xn_sc, hm_sc, gate_sc, acc_sc):
    e = pl.program_id(1)

    @pl.when(e == 0)
    def _():
        xn = x_ref[...] + _dot(mix_ref[...], wout_ref[...])
        xn_sc[...] = xn
        hm = _rms_rows(xn, gffn_ref[...])
        hm_sc[...] = hm.astype(BF16)
        gate_sc[...] = _route(hm, wr_ref, br_ref)
        acc_sc[...] = jnp.zeros_like(acc_sc)

    hb = hm_sc[...]
    gate = gate_sc[...]
    ge = jnp.sum(jnp.where(_lane_iota(gate.shape) == MOE_GROUPS + e, gate, 0.0), axis=-1, keepdims=True)
    h1 = _dot(hb, w1_ref[0])
    h3 = _dot(hb, w3_ref[0])
    a = (h1 * jax.nn.sigmoid(h1)) * h3 * ge
    acc_sc[...] += _dot(a.astype(BF16), w2_ref[0])

    @pl.when(e == MOE_EXPERTS - 1)
    def _():
        o_ref[...] = xn_sc[...] + acc_sc[...]


def _prep_moe(l, norm_ffn, moe_w_group, moe_b_group, moe_w_sub, moe_b_sub, moe_w1, moe_w3, moe_w2):
    wr = jnp.zeros((D_MODEL, LANES), F32)
    wr = wr.at[:, 0:MOE_GROUPS].set(moe_w_group[l])
    wr = wr.at[:, MOE_GROUPS:MOE_GROUPS + MOE_EXPERTS].set(
        jnp.transpose(moe_w_sub[l], (1, 0, 2)).reshape(D_MODEL, MOE_EXPERTS))
    wr_hi = wr.astype(BF16)
    wr_lo = (wr - wr_hi.astype(F32)).astype(BF16)
    br = jnp.zeros((1, LANES), F32)
    br = br.at[0, 0:MOE_GROUPS].set(moe_b_group[l])
    br = br.at[0, MOE_GROUPS:MOE_GROUPS + MOE_EXPERTS].set(moe_b_sub[l].reshape(-1))
    return dict(gffn=norm_ffn[l][None, :], wr=jnp.stack([wr_hi, wr_lo]), br=br,
                w1=moe_w1[l].astype(BF16), w3=moe_w3[l].astype(BF16), w2=moe_w2[l].astype(BF16))


def _out_moe(x2d, mix, w_out, prm, tm):
    n = x2d.shape[0]
    row = lambda w: pl.BlockSpec((tm, w), lambda i, e: (i, 0))
    cst = lambda a: pl.BlockSpec(a.shape, lambda i, e: (0,) * a.ndim)
    return pl.pallas_call(
        _out_moe_kernel,
        grid=(n // tm, MOE_EXPERTS),
        in_specs=[row(D_MODEL), row(mix.shape[1]), cst(w_out), cst(prm["gffn"]), cst(prm["wr"]), cst(prm["br"]),
                  pl.BlockSpec((1, D_MODEL, MOE_FF), lambda i, e: (e, 0, 0)),
                  pl.BlockSpec((1, D_MODEL, MOE_FF), lambda i, e: (e, 0, 0)),
                  pl.BlockSpec((1, MOE_FF, D_MODEL), lambda i, e: (e, 0, 0))],
        out_specs=row(D_MODEL),
        out_shape=jax.ShapeDtypeStruct((n, D_MODEL), F32),
        scratch_shapes=[pltpu.VMEM((tm, D_MODEL), F32), pltpu.VMEM((tm, D_MODEL), BF16),
                        pltpu.VMEM((tm, LANES), F32), pltpu.VMEM((tm, D_MODEL), F32)],
        compiler_params=pltpu.CompilerParams(dimension_semantics=("parallel", "arbitrary"),
                                             vmem_limit_bytes=VMEM_LIMIT),
        name="out_moe",
    )(x2d, mix, w_out, prm["gffn"], prm["wr"], prm["br"], prm["w1"], prm["w3"], prm["w2"])


def _proj_b_kernel(x_ref, vec_ref, win_ref, gmc_ref, wmix_ref, bmix_ref,
                   q_o, k_o, v_o, gv_o, gm_o, kmean_o, ka_o, va_o, *, tiles_per_seq):
    i = pl.program_id(0)
    x = x_ref[...]
    h = _rms_rows(x, vec_ref[0:1, :]).astype(BF16)
    z = _dot(h, win_ref[...])
    gmc = gmc_ref[...]
    q = z[:, 0:512]
    q = q * lax.rsqrt(_group_sumsq(q, gmc) * (1.0 / HEAD_DIM) + EPS) * vec_ref[1:2, 0:512]
    q_o[...] = q
    k = z[:, 512:768]
    k = k * lax.rsqrt(_group_sumsq(k, gmc) * (1.0 / HEAD_DIM) + EPS) * vec_ref[1:2, 512:768]
    k_o[...] = k
    v = z[:, 768:1024]
    v_o[...] = v
    kmean_o[0] = jnp.mean(k, axis=0, keepdims=True)
    blk = i % tiles_per_seq
    onehot = jnp.where(_lane_iota((ROW_TILE, LANES)) == 64 + blk, 1.0, 0.0)
    for hk in range(MOBA_KV_HEADS):
        ka_o[hk] = (_head_block(k, hk, HEAD_DIM) + onehot).astype(BF16)
        va_o[hk] = v[:, 64 * hk:64 * (hk + 1)].astype(BF16)

    zz = z[:, 1024:2048]
    zz = zz * (0.5 * (1.0 + jnp.tanh(math.sqrt(2.0 / math.pi) * (zz + 0.044715 * (zz * zz * zz)))))
    u = zz[:, 0:GM_WIDTH]
    g = zz[:, GM_WIDTH:]
    gc = g - jnp.mean(g, axis=-1, keepdims=True)
    gv = gc * lax.rsqrt(jnp.mean(gc * gc, axis=-1, keepdims=True) + EPS) * vec_ref[2:3, 0:512] + vec_ref[2:3, 512:1024]
    gv_o[...] = gv
    gvb = gv.astype(BF16)
    for gg in range(GM_GROUPS):
        sl = slice(GM_GROUP_DIM * gg, GM_GROUP_DIM * (gg + 1))
        mixg = _dot(wmix_ref[gg], gvb[:, sl]) + bmix_ref[gg]
        gm_o[:, sl] = (u[:, sl] * mixg).astype(BF16)


def _prep_b(i, norm_mix_l, w_in_b, moba_g_q, moba_g_k, gm_ln_g, gm_ln_b):
    row1 = jnp.concatenate([jnp.tile(moba_g_q[i], MOBA_HEADS), jnp.tile(moba_g_k[i], MOBA_KV_HEADS),
                            jnp.zeros((256,), F32)])
    row2 = jnp.concatenate([gm_ln_g[i], gm_ln_b[i]])
    g64 = np.arange(256) // 64
    gmc = (g64[:, None] == g64[None, :]).astype(np.float32)
    return dict(vec=jnp.stack([norm_mix_l, row1, row2]), win=w_in_b[i].astype(BF16), gmc=jnp.asarray(gmc, BF16))


def _gmlp_mix_weights(w_s, b_s, pos):
    cp, ch = pos % GM_CHUNK, pos // GM_CHUNK
    mask = (ch[:, None] == ch[None, :]) & (pos[None, :] <= pos[:, None])
    w = jnp.where(mask[None], w_s[:, cp[:, None], cp[None, :]], 0.0)
    b = jnp.broadcast_to(b_s[:, cp][:, :, None], (GM_GROUPS, pos.shape[0], GM_GROUP_DIM))
    return w, b


def _proj_b(x2d, prm, wmix, bmix, tiles_per_seq):
    n = x2d.shape[0]
    nt = n // ROW_TILE
    row = lambda w: pl.BlockSpec((ROW_TILE, w), lambda i: (i, 0))
    hm = lambda nh, w: pl.BlockSpec((nh, ROW_TILE, w), lambda i: (0, i, 0))
    consts = [prm["vec"], prm["win"], prm["gmc"], wmix, bmix]
    sds = jax.ShapeDtypeStruct
    return pl.pallas_call(
        functools.partial(_proj_b_kernel, tiles_per_seq=tiles_per_seq),
        grid=(nt,),
        in_specs=[row(D_MODEL)] + [_const_spec(a) for a in consts],
        out_specs=[row(512), row(256), row(256), row(GM_WIDTH), row(GM_WIDTH),
                   pl.BlockSpec((1, 1, 256), lambda i: (i, 0, 0)), hm(MOBA_KV_HEADS, LANES), hm(MOBA_KV_HEADS, 64)],
        out_shape=[sds((n, 512), F32), sds((n, 256), F32), sds((n, 256), F32), sds((n, GM_WIDTH), F32),
                   sds((n, GM_WIDTH), BF16), sds((nt, 1, 256), F32),
                   sds((MOBA_KV_HEADS, n, LANES), BF16), sds((MOBA_KV_HEADS, n, 64), BF16)],
        compiler_params=pltpu.CompilerParams(dimension_semantics=("parallel",), vmem_limit_bytes=VMEM_LIMIT),
        name="proj_b",
    )(x2d, *consts)


def _moba_gate_kernel(q_ref, km_ref, qa_o, *, tiles_per_seq):
    cur = pl.program_id(0) % tiles_per_seq
    q = q_ref[...]
    qh = q.astype(BF16)
    ql = (q - qh.astype(F32)).astype(BF16)
    gate = _dot(qh, km_ref[0, 0]) + _dot(qh, km_ref[0, 1]) + _dot(ql, km_ref[0, 0])
    lane = _lane_iota(gate.shape)
    n = lane % MOBA_MAX_BLOCKS
    valid = n < cur
    g = jnp.where(valid, gate, -jnp.inf)
    rank = jnp.zeros(gate.shape, jnp.int32)
    for j in range(1, MOBA_MAX_BLOCKS):
        wrapped = n + j >= MOBA_MAX_BLOCKS
        pv = jnp.where(wrapped, pltpu.roll(g, MOBA_MAX_BLOCKS - j, axis=1), pltpu.roll(g, LANES - j, axis=1))
        beats = jnp.logical_or(pv > g, jnp.logical_and(pv == g, wrapped))
        rank = rank + beats.astype(jnp.int32)
    sel = jnp.logical_or(jnp.logical_and(valid, rank < MOBA_TOPK), n == cur)
    sb = jnp.where(sel, 0.0, NEG_BIG)
    qs = q * ATTN_SCALE
    for hh in range(MOBA_HEADS):
        shift = (64 - MOBA_MAX_BLOCKS * hh) % LANES
        sbh = pltpu.roll(sb, shift, axis=1) if shift else sb
        sbh = jnp.where(jnp.logical_and(lane >= 64, lane < 64 + MOBA_MAX_BLOCKS), sbh, 0.0)
        qa_o[hh] = (_head_block(qs, hh, HEAD_DIM) + sbh).astype(BF16)


def _moba_km(kmean, batch, nb):
    km = kmean.reshape(batch, nb, MOBA_KV_HEADS, HEAD_DIM)
    km = jnp.repeat(km, MOBA_HEADS // MOBA_KV_HEADS, axis=2)
    km = jnp.pad(jnp.transpose(km, (0, 2, 3, 1)), ((0, 0), (0, 0), (0, 0), (0, MOBA_MAX_BLOCKS - nb)))
    km = jnp.einsum("bhdn,hg->bhdgn", km, jnp.eye(MOBA_HEADS, dtype=F32)).reshape(batch, 512, LANES)
    hi = km.astype(BF16)
    lo = (km - hi.astype(F32)).astype(BF16)
    return jnp.stack([hi, lo], axis=1)


def _moba_gate(q, km, tiles_per_seq):
    n = q.shape[0]
    return pl.pallas_call(
        functools.partial(_moba_gate_kernel, tiles_per_seq=tiles_per_seq),
        grid=(n // ROW_TILE,),
        in_specs=[pl.BlockSpec((ROW_TILE, 512), lambda i: (i, 0)),
                  pl.BlockSpec((1, 2, 512, LANES), lambda i: (i // tiles_per_seq, 0, 0, 0))],
        out_specs=pl.BlockSpec((MOBA_HEADS, ROW_TILE, LANES), lambda i: (0, i, 0)),
        out_shape=jax.ShapeDtypeStruct((MOBA_HEADS, n, LANES), BF16),
        compiler_params=pltpu.CompilerParams(dimension_semantics=("parallel",), vmem_limit_bytes=VMEM_LIMIT),
        name="moba_gate",
    )(q, km)


def _page_specs(width, pps, layer):
    return [pl.BlockSpec((None, None, LANES, width),
                         functools.partial(lambda s, j, pt, p: (layer, pt[s, j * pps + p], 0, 0), p=p))
            for p in range(pps)]


def _softmax_step(s, v, m, l, acc):
    m_new = jnp.maximum(m, jnp.max(s, axis=-1, keepdims=True))
    alpha = jnp.exp(m - m_new)
    p = jnp.exp(s - m_new)
    return m_new, alpha * l + jnp.sum(p, axis=-1, keepdims=True), alpha * acc + _dot(p.astype(BF16), v)


def _new_row_mask(shape, dec_seq):
    qidx = lax.broadcasted_iota(jnp.int32, shape, 0) % dec_seq
    return lax.broadcasted_iota(jnp.int32, shape, 1) <= qidx


def _mla_dec_kernel(pt_ref, *refs, pps, dec_seq):
    ckv_pages, kr_pages = refs[0:pps], refs[pps:2 * pps]
    (q_ref, cnew_ref, krnew_ref, wukg_ref, wukt_ref, wuv_ref, g_ref, o_ref,
     qp_sc, qr_sc, m_sc, l_sc, acc_sc) = refs[2 * pps:]
    j = pl.program_id(1)
    rows = MLA_HEADS * dec_seq

    @pl.when(j == 0)
    def _():
        for hh in range(MLA_HEADS):
            qh = q_ref[hh]
            qp_sc[dec_seq * hh:dec_seq * (hh + 1), :] = _dot(qh[:, 0:64].astype(BF16), wukg_ref[hh])
            qr_sc[dec_seq * hh:dec_seq * (hh + 1), :] = qh[:, 64:96]
        m_sc[...] = jnp.full_like(m_sc, M_INIT)
        l_sc[...] = jnp.zeros_like(l_sc)
        acc_sc[...] = jnp.zeros_like(acc_sc)

    qp = qp_sc[...].astype(BF16)
    qr = qr_sc[...].astype(BF16)
    wukt = wukt_ref[...]
    g = g_ref[...]

    def page(c, kr, carry, masked):
        cn = (c * lax.rsqrt(jnp.mean(c * c, axis=-1, keepdims=True) + EPS) * g).astype(BF16)
        kexp = _dot_nt(wukt, cn)
        ss = jnp.sum((kexp * kexp).reshape(MLA_HEADS, MLA_NOPE_DIM, LANES), axis=1)
        r = lax.rsqrt(ss * (1.0 / MLA_NOPE_DIM) + EPS)
        s = _dot_nt(qp, cn).reshape(MLA_HEADS, dec_seq, LANES) * r[:, None, :]
        s = s.reshape(rows, LANES) + _dot_nt(qr, kr.astype(BF16))
        if masked:
            s = jnp.where(_new_row_mask(s.shape, dec_seq), s, -jnp.inf)
        return _softmax_step(s, cn, *carry)

    carry = (m_sc[...], l_sc[...], acc_sc[...])
    for p in range(pps):
        carry = page(ckv_pages[p][...], kr_pages[p][...], carry, False)
    m_sc[...], l_sc[...], acc_sc[...] = carry

    @pl.when(j == pl.num_programs(1) - 1)
    def _():
        m, l, acc = page(cnew_ref[0], krnew_ref[0], (m_sc[...], l_sc[...], acc_sc[...]), True)
        lat = acc / l
        for hh in range(MLA_HEADS):
            lat_h = lat[dec_seq * hh:dec_seq * (hh + 1), :].astype(BF16)
            o_ref[0, :, 64 * hh:64 * (hh + 1)] = _dot(lat_h, wuv_ref[hh])


def _mla_decode(cache_ckv, cache_kr, layer, page_table, qm, cnew, krnew, wukg, wukt, wuv, g_ckv, pps, dec_seq):
    n_seq, n_pages = page_table.shape
    rows = MLA_HEADS * dec_seq
    per_seq = lambda shp: pl.BlockSpec((1,) + shp, lambda s, j, pt: (s,) + (0,) * len(shp))
    cst = lambda a: pl.BlockSpec(a.shape, lambda s, j, pt: (0,) * a.ndim)
    grid_spec = pltpu.PrefetchScalarGridSpec(
        num_scalar_prefetch=1, grid=(n_seq, n_pages // pps),
        in_specs=(_page_specs(MLA_KV_RANK, pps, layer) + _page_specs(MLA_ROPE_DIM, pps, layer)
                  + [pl.BlockSpec((MLA_HEADS, dec_seq, LANES), lambda s, j, pt: (0, s, 0)),
                     per_seq((LANES, MLA_KV_RANK)), per_seq((LANES, MLA_ROPE_DIM)),
                     cst(wukg), cst(wukt), cst(wuv), cst(g_ckv)]),
        out_specs=per_seq((dec_seq, MLA_HEADS * MLA_V_DIM)),
        scratch_shapes=[pltpu.VMEM((rows, MLA_KV_RANK), F32), pltpu.VMEM((rows, MLA_ROPE_DIM), F32),
                        pltpu.VMEM((rows, 1), F32), pltpu.VMEM((rows, 1), F32), pltpu.VMEM((rows, MLA_KV_RANK), F32)])
    return pl.pallas_call(
        functools.partial(_mla_dec_kernel, pps=pps, dec_seq=dec_seq),
        grid_spec=grid_spec,
        out_shape=jax.ShapeDtypeStruct((n_seq, dec_seq, MLA_HEADS * MLA_V_DIM), F32),
        compiler_params=pltpu.CompilerParams(dimension_semantics=("parallel", "arbitrary"),
                                             vmem_limit_bytes=VMEM_LIMIT),
        name="mla_decode",
    )(page_table, *([cache_ckv] * pps), *([cache_kr] * pps), qm, cnew, krnew, wukg, wukt, wuv, g_ckv)


def _fox_dec_kernel(pt_ref, *refs, pps, dec_seq):
    k_pages, v_pages, lf_pages = refs[0:pps], refs[pps:2 * pps], refs[2 * pps:3 * pps]
    (qbd_ref, knew_ref, vnew_ref, lfnew_ref, utri_ref, o_ref, m_sc, l_sc, acc_sc, base_sc) = refs[3 * pps:]
    j = pl.program_id(1)
    rows = FOX_HEADS * dec_seq

    @pl.when(j == 0)
    def _():
        m_sc[...] = jnp.full_like(m_sc, M_INIT)
        l_sc[...] = jnp.zeros_like(l_sc)
        acc_sc[...] = jnp.zeros_like(acc_sc)
        base_sc[...] = jnp.zeros_like(base_sc)

    qbd = qbd_ref[0]
    utri = utri_ref[...]

    def cum_lanes(lf):
        a, b, c = _split3(lf)
        return _dot(a, utri) + _dot(b, utri) + _dot(c, utri)

    def page(k, v, c_p, carry, masked):
        s = _dot_nt(qbd, k.astype(BF16)).reshape(FOX_HEADS, dec_seq, LANES) - c_p[:, None, :]
        s = s.reshape(rows, LANES)
        if masked:
            s = jnp.where(_new_row_mask(s.shape, dec_seq), s, -jnp.inf)
        return _softmax_step(s, v.astype(BF16), *carry)

    cum = cum_lanes(jnp.concatenate([lf_pages[p][...] for p in range(pps)], axis=0))
    base = base_sc[...]
    carry = (m_sc[...], l_sc[...], acc_sc[...])
    for p in range(pps):
        c_loc = cum[FOX_HEADS * p:FOX_HEADS * (p + 1), :]
        carry = page(k_pages[p][...], v_pages[p][...], c_loc + base, carry, False)
        base = base + c_loc[:, LANES - 1:LANES]
    base_sc[...] = base
    m_sc[...], l_sc[...], acc_sc[...] = carry

    @pl.when(j == pl.num_programs(1) - 1)
    def _():
        c_new = cum_lanes(lfnew_ref[0]) + base_sc[...]
        m, l, acc = page(knew_ref[0], vnew_ref[0], c_new, (m_sc[...], l_sc[...], acc_sc[...]), True)
        o = acc / l
        for hq in range(FOX_HEADS):
            hk = hq // 2
            o_ref[0, :, 64 * hq:64 * (hq + 1)] = o[dec_seq * hq:dec_seq * (hq + 1), 64 * hk:64 * (hk + 1)]


def _fox_decode(cache_k, cache_v, cache_lft, layer, page_table, qbd, knew, vnew, lfnew, pps, dec_seq):
    n_seq, n_pages = page_table.shape
    rows = FOX_HEADS * dec_seq
    per_seq = lambda shp: pl.BlockSpec((1,) + shp, lambda s, j, pt: (s,) + (0,) * len(shp))
    utri = jnp.asarray(np.triu(np.ones((LANES, LANES), np.float32)), BF16)
    lf_specs = [pl.BlockSpec((None, None, FOX_HEADS, LANES),
                             functools.partial(lambda s, j, pt, p: (layer, pt[s, j * pps + p], 0, 0), p=p))
                for p in range(pps)]
    grid_spec = pltpu.PrefetchScalarGridSpec(
        num_scalar_prefetch=1, grid=(n_seq, n_pages // pps),
        in_specs=(_page_specs(256, pps, layer) + _page_specs(256, pps, layer) + lf_specs
                  + [per_seq((rows, 256)), per_seq((LANES, 256)), per_seq((LANES, 256)), per_seq((FOX_HEADS, LANES)),
                     pl.BlockSpec((LANES, LANES), lambda s, j, pt: (0, 0))]),
        out_specs=per_seq((dec_seq, FOX_HEADS * HEAD_DIM)),
        scratch_shapes=[pltpu.VMEM((rows, 1), F32), pltpu.VMEM((rows, 1), F32), pltpu.VMEM((rows, 256), F32),
                        pltpu.VMEM((FOX_HEADS, 1), F32)])
    return pl.pallas_call(
        functools.partial(_fox_dec_kernel, pps=pps, dec_seq=dec_seq),
        grid_spec=grid_spec,
        out_shape=jax.ShapeDtypeStruct((n_seq, dec_seq, FOX_HEADS * HEAD_DIM), F32),
        compiler_params=pltpu.CompilerParams(dimension_semantics=("parallel", "arbitrary"),
                                             vmem_limit_bytes=VMEM_LIMIT),
        name="fox_decode",
    )(page_table, *([cache_k] * pps), *([cache_v] * pps), *([cache_lft] * pps), qbd, knew, vnew, lfnew, utri)


def _moba_dec_kernel(pt_ref, *refs, pps, dec_seq, nb):
    k_pages, v_pages = refs[0:pps], refs[pps:2 * pps]
    (qbd_ref, qg_ref, knew_ref, vnew_ref, o_ref, m_sc, l_sc, gate_sc, o_sc) = refs[2 * pps:]
    j = pl.program_id(1)
    rows = MOBA_HEADS * dec_seq
    bps = pps // 2
    qbd = qbd_ref[0]
    qg = qg_ref[0]
    lane = _lane_iota((rows, LANES))

    @pl.when(j == 0)
    def _():
        m_sc[...] = jnp.zeros_like(m_sc)
        l_sc[...] = jnp.zeros_like(l_sc)
        gate_sc[...] = jnp.zeros_like(gate_sc)

    m_all, l_all, g_all = m_sc[...], l_sc[...], gate_sc[...]
    for b in range(bps):
        n = j * bps + b
        k = jnp.concatenate([k_pages[2 * b][...], k_pages[2 * b + 1][...]], axis=0)
        v = jnp.concatenate([v_pages[2 * b][...], v_pages[2 * b + 1][...]], axis=0)
        s = _dot_nt(qbd, k.astype(BF16))
        mn = jnp.max(s, axis=-1, keepdims=True)
        p = jnp.exp(s - mn)
        o_sc[n] = _dot(p.astype(BF16), v.astype(BF16))
        gn = jnp.sum(qg * jnp.mean(k, axis=0, keepdims=True), axis=-1, keepdims=True)
        here = lane == n
        m_all = jnp.where(here, mn, m_all)
        l_all = jnp.where(here, jnp.sum(p, axis=-1, keepdims=True), l_all)
        g_all = jnp.where(here, gn, g_all)
    m_sc[...], l_sc[...], gate_sc[...] = m_all, l_all, g_all

    @pl.when(j == pl.num_programs(1) - 1)
    def _():
        s = _dot_nt(qbd, knew_ref[0].astype(BF16))
        s = jnp.where(_new_row_mask(s.shape, dec_seq), s, -jnp.inf)
        m_o = jnp.max(s, axis=-1, keepdims=True)
        p = jnp.exp(s - m_o)
        l_o = jnp.sum(p, axis=-1, keepdims=True)
        o_o = _dot(p.astype(BF16), vnew_ref[0].astype(BF16))
        g = jnp.where(lane < nb, g_all, -jnp.inf)
        sel = jnp.zeros(g.shape, jnp.bool_)
        for _ in range(min(MOBA_TOPK, nb)):
            vmax, idx = _first_index_of_max(g, lane)
            hit = lane == idx
            sel = jnp.logical_or(sel, jnp.logical_and(hit, vmax > -jnp.inf))
            g = jnp.where(hit, -jnp.inf, g)
        m_top = jnp.maximum(jnp.max(jnp.where(sel, m_all, -jnp.inf), axis=-1, keepdims=True), m_o)
        w = jnp.where(sel, jnp.exp(m_all - m_top), 0.0)
        w_o = jnp.exp(m_o - m_top)
        den = jnp.sum(w * l_all, axis=-1, keepdims=True) + w_o * l_o

        def body(nn, acc):
            col = jnp.sum(jnp.where(lane == nn, w, 0.0), axis=-1, keepdims=True)
            return acc + col * o_sc[nn]

        o = lax.fori_loop(0, nb, body, w_o * o_o) / den
        for hq in range(MOBA_HEADS):
            hk = hq // 2
            o_ref[0, :, 64 * hq:64 * (hq + 1)] = o[dec_seq * hq:dec_seq * (hq + 1), 64 * hk:64 * (hk + 1)]


def _moba_decode(cache_k, cache_v, layer, page_table, qbd, qg, knew, vnew, pps, dec_seq):
    n_seq, n_pages = page_table.shape
    rows = MOBA_HEADS * dec_seq
    nb = n_pages // 2
    per_seq = lambda shp: pl.BlockSpec((1,) + shp, lambda s, j, pt: (s,) + (0,) * len(shp))
    grid_spec = pltpu.PrefetchScalarGridSpec(
        num_scalar_prefetch=1, grid=(n_seq, n_pages // pps),
        in_specs=(_page_specs(256, pps, layer) + _page_specs(256, pps, layer)
                  + [per_seq((rows, 256)), per_seq((rows, 256)), per_seq((LANES, 256)), per_seq((LANES, 256))]),
        out_specs=per_seq((dec_seq, MOBA_HEADS * HEAD_DIM)),
        scratch_shapes=[pltpu.VMEM((rows, LANES), F32), pltpu.VMEM((rows, LANES), F32), pltpu.VMEM((rows, LANES), F32),
                        pltpu.VMEM((nb, rows, 256), F32)])
    return pl.pallas_call(
        functools.partial(_moba_dec_kernel, pps=pps, dec_seq=dec_seq, nb=nb),
        grid_spec=grid_spec,
        out_shape=jax.ShapeDtypeStruct((n_seq, dec_seq, MOBA_HEADS * HEAD_DIM), F32),
        compiler_params=pltpu.CompilerParams(dimension_semantics=("parallel", "arbitrary"),
                                             vmem_limit_bytes=VMEM_LIMIT),
        name="moba_decode",
    )(page_table, *([cache_k] * pps), *([cache_v] * pps), qbd, qg, knew, vnew)


def _block_diag_queries(q4):
    nh = q4.shape[0]
    place = jnp.asarray(np.arange(nh)[:, None] // 2 == np.arange(nh // 2)[None, :], q4.dtype)
    out = jnp.einsum("hsqd,hk->shqkd", q4, place)
    return out.reshape(q4.shape[1], nh * q4.shape[2], (nh // 2) * 64)


def _pad_rows(a, rows):
    return jnp.pad(a, ((0, 0), (0, rows - a.shape[1]), (0, 0)))


def kernel(x_prompt, x_sample, cache_mla_ckv, cache_mla_krope, cache_fox_k, cache_fox_v, cache_fox_logf, cache_moba_k, cache_moba_v, page_table, norm_mix, norm_ffn, w_in_a, mla_g_cq, mla_w_uq, mla_g_ckv, mla_w_ukv, mla_g_qn, mla_g_qr, mla_g_kn, mla_g_kr, fox_g_q, fox_g_k, fox_b_f, w_out_a, w_in_b, moba_g_q, moba_g_k, gm_ln_g, gm_ln_b, gm_w_s, gm_b_s, w_out_b, moe_w_group, moe_b_group, moe_w_sub, moe_b_sub, moe_w1, moe_w3, moe_w2):
    B, T, _ = x_prompt.shape
    S, DS, _ = x_sample.shape
    n_pages = page_table.shape[1]
    page = cache_mla_ckv.shape[2]
    past = n_pages * page
    n_phys = cache_mla_ckv.shape[1]
    assert page == LANES and T % 1024 == 0 and T // MOBA_BLOCK <= MOBA_MAX_BLOCKS
    assert (S * DS) % ROW_TILE == 0 and ROW_TILE % DS == 0 and DS <= LANES
    assert past % MOBA_BLOCK == 0 and past % GM_CHUNK == 0 and n_pages % 2 == 0
    pps = math.gcd(16, n_pages)
    tps = T // ROW_TILE
    np_rows, ns_rows = B * T, S * DS
    xp = x_prompt.reshape(np_rows, D_MODEL)
    xs = x_sample.reshape(ns_rows, D_MODEL)
    pos_s = past + jnp.arange(ROW_TILE) % DS

    pa = _prep_a(0, norm_mix[0], w_in_a, mla_g_cq, mla_w_uq, mla_g_ckv, mla_w_ukv, mla_g_qn, mla_g_qr, mla_g_kn,
                 mla_g_kr, fox_g_q, fox_g_k, fox_b_f)
    p_ckv, p_kr, p_fk, p_fv, p_fl, qm, km, vm, qf, kf, vf = _proj_a(xp, pa, _rot_tables(jnp.arange(T)), tps)
    s_ckv, s_kr, s_fk, s_fv, s_fl, qm_s, _, _, qf_s, _, _ = _proj_a(xs, pa, _rot_tables(pos_s), 1)
    mix_p = jnp.concatenate([_flash(qm, km, vm, B, T, tq=1024, tk=1024),
                             _flash(qf, kf, vf, B, T, tq=1024, tk=1024)], axis=1)

    wukv = mla_w_ukv[0].reshape(MLA_KV_RANK, MLA_HEADS, 128)
    wuk_t = jnp.transpose(wukv[:, :, 0:64], (1, 2, 0))
    wukg = (wuk_t * mla_g_kn[0][None, :, None]).astype(BF16)
    wukt = wuk_t.reshape(MLA_HEADS * 64, MLA_KV_RANK).astype(BF16)
    wuv = jnp.transpose(wukv[:, :, 64:128], (1, 0, 2)).astype(BF16)
    o_mla = _mla_decode(cache_mla_ckv, cache_mla_krope, 0, page_table, qm_s.astype(F32),
                        _pad_rows(s_ckv.reshape(S, DS, -1), LANES), _pad_rows(s_kr.reshape(S, DS, -1), LANES),
                        wukg, wukt, wuv, mla_g_ckv[0][None, :], pps, DS)
    qbd_f = _block_diag_queries(qf_s[:, :, 0:64].reshape(FOX_HEADS, S, DS, 64))
    lft = jnp.transpose(cache_fox_logf, (0, 1, 3, 2))
    lfnew = jnp.pad(jnp.transpose(s_fl.reshape(S, DS, FOX_HEADS), (0, 2, 1)), ((0, 0), (0, 0), (0, LANES - DS)))
    o_fox = _fox_decode(cache_fox_k.reshape(-1, n_phys, page, 256), cache_fox_v.reshape(-1, n_phys, page, 256), lft,
                        0, page_table, qbd_f, _pad_rows(s_fk.reshape(S, DS, -1), LANES),
                        _pad_rows(s_fv.reshape(S, DS, -1), LANES), lfnew, pps, DS)
    mix_s = jnp.concatenate([o_mla.reshape(ns_rows, -1), o_fox.reshape(ns_rows, -1)], axis=1).astype(BF16)

    pm = _prep_moe(0, norm_ffn, moe_w_group, moe_b_group, moe_w_sub, moe_b_sub, moe_w1, moe_w3, moe_w2)
    w_out = w_out_a[0].astype(BF16)
    xp = _out_moe(xp, mix_p, w_out, pm, min(1024, np_rows))
    xs = _out_moe(xs, mix_s, w_out, pm, min(1024, ns_rows))

    pb = _prep_b(0, norm_mix[1], w_in_b, moba_g_q, moba_g_k, gm_ln_g, gm_ln_b)
    wmix_p, bmix_p = _gmlp_mix_weights(gm_w_s[0], gm_b_s[0], jnp.arange(ROW_TILE))
    wmix_s, bmix_s = _gmlp_mix_weights(gm_w_s[0], gm_b_s[0], pos_s)
    same_seq = (np.arange(ROW_TILE)[:, None] // DS == np.arange(ROW_TILE)[None, :] // DS)
    wmix_s = jnp.where(same_seq[None], wmix_s, 0.0)
    q_p, p_mk, p_mv, _, gm_p, kmean, ka, va = _proj_b(xp, pb, wmix_p.astype(BF16), bmix_p, tps)
    q_s, s_mk, s_mv, s_gv, gm_s, _, _, _ = _proj_b(xs, pb, wmix_s.astype(BF16), bmix_s, 1)
    qa = _moba_gate(q_p, _moba_km(kmean, B, tps), tps)
    mix_p = jnp.concatenate([_flash(qa, ka, va, B, T, tq=1024, tk=1024), gm_p], axis=1)

    q4 = jnp.transpose(q_s.reshape(S, DS, MOBA_HEADS, 64), (2, 0, 1, 3))
    o_moba = _moba_decode(cache_moba_k.reshape(-1, n_phys, page, 256), cache_moba_v.reshape(-1, n_phys, page, 256), 0,
                          page_table, _block_diag_queries((q4 * ATTN_SCALE).astype(BF16)), _block_diag_queries(q4),
                          _pad_rows(s_mk.reshape(S, DS, -1), LANES), _pad_rows(s_mv.reshape(S, DS, -1), LANES), pps, DS)
    mix_s = jnp.concatenate([o_moba.reshape(ns_rows, -1).astype(BF16), gm_s], axis=1)

    pm = _prep_moe(1, norm_ffn, moe_w_group, moe_b_group, moe_w_sub, moe_b_sub, moe_w1, moe_w3, moe_w2)
    w_out = w_out_b[0].astype(BF16)
    xp = _out_moe(xp, mix_p, w_out, pm, min(1024, np_rows))
    xs = _out_moe(xs, mix_s, w_out, pm, min(1024, ns_rows))

    pr = lambda a, *tail: a.reshape((1, B, T) + tail)
    sr = lambda a, *tail: a.reshape((1, S, DS) + tail)
    return (xp.reshape(B, T, D_MODEL), xs.reshape(S, DS, D_MODEL),
            pr(p_ckv, MLA_KV_RANK), pr(p_kr, MLA_ROPE_DIM), pr(p_fk, FOX_KV_HEADS, 64), pr(p_fv, FOX_KV_HEADS, 64),
            pr(p_fl, FOX_HEADS), pr(p_mk, MOBA_KV_HEADS, 64), pr(p_mv, MOBA_KV_HEADS, 64),
            sr(s_ckv, MLA_KV_RANK), sr(s_kr, MLA_ROPE_DIM), sr(s_fk, FOX_KV_HEADS, 64), sr(s_fv, FOX_KV_HEADS, 64),
            sr(s_fl, FOX_HEADS), sr(s_mk, MOBA_KV_HEADS, 64), sr(s_mv, MOBA_KV_HEADS, 64), sr(s_gv, GM_WIDTH))
```

```python
import functools
import math

import jax
import jax.numpy as jnp
import numpy as np
from jax import lax
from jax.experimental import pallas as pl
from jax.experimental.pallas import tpu as pltpu

F32 = jnp.float32
BF16 = jnp.bfloat16

D_MODEL = 1024
HEAD_DIM = 64
MLA_HEADS = 8
MLA_Q_RANK = 256
MLA_KV_RANK = 256
MLA_NOPE_DIM = 64
MLA_ROPE_DIM = 32
MLA_V_DIM = 64
ROPE_THETA = 10000.0
FOX_HEADS = 8
FOX_KV_HEADS = 4
MOBA_HEADS = 8
MOBA_KV_HEADS = 4
MOBA_BLOCK = 256
MOBA_TOPK = 3
GM_GROUPS = 4
GM_GROUP_DIM = 128
GM_WIDTH = GM_GROUPS * GM_GROUP_DIM
GM_CHUNK = 128
MOE_GROUPS = 4
MOE_PER_GROUP = 4
MOE_EXPERTS = 16
MOE_FF = 256
EPS = 1e-6
MLA_SCALE = (MLA_NOPE_DIM + MLA_ROPE_DIM) ** -0.5
ATTN_SCALE = HEAD_DIM ** -0.5

LANES = 128
ROW_TILE = 256
MOBA_MAX_BLOCKS = 16
NEG_BIG = -(2.0 ** 100)
M_INIT = -1e30
VMEM_LIMIT = 48 * 1024 * 1024


def _dot(a, b):
    return jnp.dot(a, b, preferred_element_type=F32)


def _dot_nt(a, b):
    return lax.dot_general(a, b, (((1,), (1,)), ((), ())), preferred_element_type=F32)


def _split3(x):
    hi = x.astype(BF16)
    r = x - hi.astype(F32)
    mid = r.astype(BF16)
    lo = (r - mid.astype(F32)).astype(BF16)
    return hi, mid, lo


def _lane_iota(shape):
    return lax.broadcasted_iota(jnp.int32, shape, len(shape) - 1)


def _head_block(x, h, width):
    start = h * width
    blk = x[:, (start // LANES) * LANES:(start // LANES + 1) * LANES]
    off = start % LANES
    if off:
        blk = pltpu.roll(blk, LANES - off, axis=1)
    return jnp.where(_lane_iota(blk.shape) < width, blk, 0.0)


def _group_sumsq(x, gmat):
    outs = []
    for j in range(x.shape[1] // 256):
        xs = x[:, 256 * j:256 * (j + 1)]
        outs.append(_dot((xs * xs).astype(BF16), gmat))
    return outs[0] if len(outs) == 1 else jnp.concatenate(outs, axis=1)


def _rms_rows(x, g):
    return x * lax.rsqrt(jnp.mean(x * x, axis=-1, keepdims=True) + EPS) * g


def _rotate(x, rot_ref):
    c, s1, s2 = rot_ref[0], rot_ref[1], rot_ref[2]
    return x * c + pltpu.roll(x, 16, axis=1) * s1 + pltpu.roll(x, LANES - 16, axis=1) * s2


def _proj_a_kernel(x_ref, vec_ref, win_ref, wuq_ref, wuk_ref, wuv_ref, gmq_ref, gmc_ref, rot_ref,
                   tri_ref, pcum_ref,
                   ckv_o, kr_o, fk_o, fv_o, fl_o, qm_o, km_o, vm_o, qf_o, kf_o, vf_o,
                   carry_sc, *, tiles_per_seq):
    i = pl.program_id(0)
    x = x_ref[...]
    h = _rms_rows(x, vec_ref[0:1, :]).astype(BF16)
    z = _dot(h, win_ref[...])

    cqn = _rms_rows(z[:, 0:256], vec_ref[1:2, 0:256]).astype(BF16)
    c_kv = z[:, 256:512]
    ckv_o[...] = c_kv
    ckvn = _rms_rows(c_kv, vec_ref[1:2, 256:512]).astype(BF16)

    krb = z[:, 512:640]
    krb = krb * lax.rsqrt(jnp.sum(krb * krb, axis=-1, keepdims=True) * (1.0 / MLA_ROPE_DIM) + EPS)
    krb = _rotate(krb * vec_ref[1:2, 512:640], rot_ref)
    kr_o[...] = krb[:, 64:96]

    gmq = gmq_ref[...]
    q = _dot(cqn, wuq_ref[...])
    q = q * lax.rsqrt(_group_sumsq(q, gmq) * vec_ref[3:4, :] + EPS) * vec_ref[2:3, :]
    kx = _dot(ckvn, wuk_ref[...])
    kx = kx * lax.rsqrt(_group_sumsq(kx, gmq) * vec_ref[3:4, :] + EPS) * vec_ref[4:5, :]
    vx = _dot(ckvn, wuv_ref[...])
    for hh in range(MLA_HEADS):
        qb = q[:, LANES * hh:LANES * (hh + 1)]
        qm_o[hh] = _rotate(qb, rot_ref).astype(BF16)
        km_o[hh] = (kx[:, LANES * hh:LANES * (hh + 1)] + krb).astype(BF16)
        vm_o[hh] = vx[:, 64 * hh:64 * (hh + 1)].astype(BF16)

    gmc = gmc_ref[...]
    fq = z[:, 640:1152]
    fq = fq * lax.rsqrt(_group_sumsq(fq, gmc) * (1.0 / HEAD_DIM) + EPS) * vec_ref[5:6, 0:512]
    fk = z[:, 1152:1408]
    fk = fk * lax.rsqrt(_group_sumsq(fk, gmc) * (1.0 / HEAD_DIM) + EPS) * vec_ref[5:6, 512:768]
    fk_o[...] = fk
    fv = z[:, 1408:1664]
    fv_o[...] = fv
    f = z[:, 1664:1792] + vec_ref[1:2, 640:768]
    logf = jnp.minimum(f, 0.0) - jnp.log1p(jnp.exp(-jnp.abs(f)))
    fl_o[...] = logf[:, 0:FOX_HEADS]

    @pl.when(i % tiles_per_seq == 0)
    def _():
        carry_sc[...] = jnp.zeros_like(carry_sc)

    tri = tri_ref[...]
    lh, lm, ll = _split3(logf)
    cum = _dot(tri, lh) + _dot(tri, lm) + _dot(tri, ll) + carry_sc[...]
    carry_sc[...] = cum[ROW_TILE - 1:ROW_TILE, :]
    ch, cm, cl = _split3(cum)
    aug = _dot(ch, pcum_ref[0]) + _dot(cm, pcum_ref[1]) + _dot(cl, pcum_ref[2])

    for hh in range(FOX_HEADS):
        g = hh % 2
        qf_o[hh] = (_head_block(fq, hh, HEAD_DIM) + vec_ref[6 + g:7 + g, 0:LANES]).astype(BF16)
    for hk in range(FOX_KV_HEADS):
        kf_o[hk] = (_head_block(fk, hk, HEAD_DIM) + aug[:, LANES * hk:LANES * (hk + 1)]).astype(BF16)
        vf_o[hk] = fv[:, 64 * hk:64 * (hk + 1)].astype(BF16)


def _rot_tables(pos):
    half = MLA_ROPE_DIM // 2
    inv = jnp.exp(jnp.arange(half, dtype=F32) * (-2.0 * math.log(ROPE_THETA) / MLA_ROPE_DIM))
    ang = pos.astype(F32)[:, None] * inv[None, :]
    cos, sin = jnp.cos(ang), jnp.sin(ang)
    n = pos.shape[0]
    c = jnp.ones((n, LANES), F32).at[:, 64:80].set(cos).at[:, 80:96].set(cos)
    s1 = jnp.zeros((n, LANES), F32).at[:, 80:96].set(sin)
    s2 = jnp.zeros((n, LANES), F32).at[:, 64:80].set(-sin)
    return jnp.stack([c, s1, s2])


def _prep_a(i, norm_mix_l, w_in_a, mla_g_cq, mla_w_uq, mla_g_ckv, mla_w_ukv, mla_g_qn, mla_g_qr, mla_g_kn,
            mla_g_kr, fox_g_q, fox_g_k, fox_b_f):
    w = w_in_a[i]
    z128 = jnp.zeros((D_MODEL, LANES), F32)
    w_kr = z128.at[:, 64:96].set(w[:, 512:544])
    w_fl = z128.at[:, 0:FOX_HEADS].set(w[:, 1568:1576])
    win = jnp.concatenate([w[:, 0:512], w_kr, w[:, 544:1568], w_fl], axis=1).astype(BF16)

    wuq = mla_w_uq[i].reshape(MLA_Q_RANK, MLA_HEADS, 96)
    wuq = jnp.pad(wuq, ((0, 0), (0, 0), (0, 32))).reshape(MLA_Q_RANK, MLA_HEADS * LANES).astype(BF16)
    wukv = mla_w_ukv[i].reshape(MLA_KV_RANK, MLA_HEADS, 128)
    wuk = jnp.pad(wukv[:, :, 0:64], ((0, 0), (0, 0), (0, 64))).reshape(MLA_KV_RANK, MLA_HEADS * LANES).astype(BF16)
    wuv = wukv[:, :, 64:128].reshape(MLA_KV_RANK, MLA_HEADS * 64).astype(BF16)

    lane = np.arange(256) % LANES
    grp = np.where(lane < 64, 0, np.where(lane < 96, 1, -1)) + 2 * (np.arange(256) // LANES)
    valid = (lane < 96)
    gmq = ((grp[:, None] == grp[None, :]) & valid[:, None] & valid[None, :]).astype(np.float32)
    g64 = np.arange(256) // 64
    gmc = (g64[:, None] == g64[None, :]).astype(np.float32)

    zero32 = jnp.zeros((32,), F32)
    zero64 = jnp.zeros((64,), F32)
    qgain = jnp.tile(jnp.concatenate([mla_g_qn[i], mla_g_qr[i], zero32]) * MLA_SCALE, MLA_HEADS)
    invw = jnp.tile(jnp.concatenate([jnp.full((64,), 1 / 64., F32), jnp.full((32,), 1 / 32., F32), zero32]),
                    MLA_HEADS)
    kgain = jnp.tile(jnp.concatenate([mla_g_kn[i], zero64]), MLA_HEADS)
    g_kr_blk = jnp.zeros((LANES,), F32).at[64:96].set(mla_g_kr[i])
    b_f_blk = jnp.zeros((LANES,), F32).at[0:FOX_HEADS].set(fox_b_f[i])
    row1 = jnp.concatenate([mla_g_cq[i], mla_g_ckv[i], g_kr_blk, b_f_blk, jnp.zeros((256,), F32)])
    row5 = jnp.concatenate([jnp.tile(fox_g_q[i], FOX_HEADS) * ATTN_SCALE, jnp.tile(fox_g_k[i], FOX_KV_HEADS),
                            jnp.zeros((256,), F32)])
    qa0 = jnp.zeros((D_MODEL,), F32).at[64:67].set(1.0)
    qa1 = jnp.zeros((D_MODEL,), F32).at[67:70].set(1.0)
    vec = jnp.stack([norm_mix_l, row1, qgain, invw, kgain, row5, qa0, qa1])

    tri = np.tril(np.ones((ROW_TILE, ROW_TILE), np.float32))
    pcum = np.zeros((3, LANES, FOX_KV_HEADS * LANES), np.float32)
    for hk in range(FOX_KV_HEADS):
        for g in range(2):
            for j in range(3):
                pcum[j, 2 * hk + g, LANES * hk + 64 + 3 * g + j] = -1.0
    return dict(vec=vec, win=win, wuq=wuq, wuk=wuk, wuv=wuv, gmq=jnp.asarray(gmq, BF16),
                gmc=jnp.asarray(gmc, BF16), tri=jnp.asarray(tri, BF16), pcum=jnp.asarray(pcum, BF16))


def _const_spec(a):
    nd = a.ndim
    return pl.BlockSpec(a.shape, lambda *_: (0,) * nd)


def _proj_a(x2d, prm, rot, tiles_per_seq):
    n = x2d.shape[0]
    nt = n // ROW_TILE
    rt = rot.shape[1] // ROW_TILE
    row = lambda w: pl.BlockSpec((ROW_TILE, w), lambda i: (i, 0))
    hm = lambda nh, w: pl.BlockSpec((nh, ROW_TILE, w), lambda i: (0, i, 0))
    consts = [prm[k] for k in ("vec", "win", "wuq", "wuk", "wuv", "gmq", "gmc")]
    in_specs = ([row(D_MODEL)] + [_const_spec(a) for a in consts]
                + [pl.BlockSpec((3, ROW_TILE, LANES), lambda i: (0, i % rt, 0)),
                   _const_spec(prm["tri"]), _const_spec(prm["pcum"])])
    out_shape = [jax.ShapeDtypeStruct((n, 256), F32), jax.ShapeDtypeStruct((n, MLA_ROPE_DIM), F32),
                 jax.ShapeDtypeStruct((n, 256), F32), jax.ShapeDtypeStruct((n, 256), F32),
                 jax.ShapeDtypeStruct((n, FOX_HEADS), F32),
                 jax.ShapeDtypeStruct((MLA_HEADS, n, LANES), BF16), jax.ShapeDtypeStruct((MLA_HEADS, n, LANES), BF16),
                 jax.ShapeDtypeStruct((MLA_HEADS, n, 64), BF16),
                 jax.ShapeDtypeStruct((FOX_HEADS, n, LANES), BF16), jax.ShapeDtypeStruct((FOX_KV_HEADS, n, LANES), BF16),
                 jax.ShapeDtypeStruct((FOX_KV_HEADS, n, 64), BF16)]
    out_specs = [row(256), row(MLA_ROPE_DIM), row(256), row(256), row(FOX_HEADS),
                 hm(MLA_HEADS, LANES), hm(MLA_HEADS, LANES), hm(MLA_HEADS, 64),
                 hm(FOX_HEADS, LANES), hm(FOX_KV_HEADS, LANES), hm(FOX_KV_HEADS, 64)]
    return pl.pallas_call(
        functools.partial(_proj_a_kernel, tiles_per_seq=tiles_per_seq),
        grid=(nt,), in_specs=in_specs, out_specs=out_specs, out_shape=out_shape,
        scratch_shapes=[pltpu.VMEM((1, LANES), F32)],
        compiler_params=pltpu.CompilerParams(dimension_semantics=("arbitrary",), vmem_limit_bytes=VMEM_LIMIT),
        name="proj_a",
    )(x2d, *consts, rot, prm["tri"], prm["pcum"])


def _flash_kernel(q_ref, k_ref, v_ref, o_ref, m_sc, l_sc, acc_sc, *, tq, tk, kv_shared):
    qi = pl.program_id(2)
    ki = pl.program_id(3)
    last_k = (qi * tq + tq - 1) // tk

    @pl.when(ki == 0)
    def _():
        m_sc[...] = jnp.full_like(m_sc, M_INIT)
        l_sc[...] = jnp.zeros_like(l_sc)
        acc_sc[...] = jnp.zeros_like(acc_sc)

    def step(masked):
        for hh in range(2):
            kvh = 0 if kv_shared else hh
            s = _dot_nt(q_ref[hh], k_ref[kvh])
            if masked:
                rows = qi * tq + lax.broadcasted_iota(jnp.int32, s.shape, 0)
                cols = ki * tk + lax.broadcasted_iota(jnp.int32, s.shape, 1)
                s = jnp.where(cols <= rows, s, -jnp.inf)
            m_old = m_sc[hh]
            m_new = jnp.maximum(m_old, jnp.max(s, axis=-1, keepdims=True))
            alpha = jnp.exp(m_old - m_new)
            p = jnp.exp(s - m_new)
            l_sc[hh] = alpha * l_sc[hh] + jnp.sum(p, axis=-1, keepdims=True)
            acc_sc[hh] = alpha * acc_sc[hh] + _dot(p.astype(BF16), v_ref[kvh])
            m_sc[hh] = m_new

    needs_mask = ki * tk + tk - 1 > qi * tq

    @pl.when(jnp.logical_and(ki <= last_k, needs_mask))
    def _():
        step(True)

    @pl.when(jnp.logical_and(ki <= last_k, jnp.logical_not(needs_mask)))
    def _():
        step(False)

    @pl.when(ki == last_k)
    def _():
        for hh in range(2):
            o_ref[:, 64 * hh:64 * (hh + 1)] = (acc_sc[hh] / l_sc[hh]).astype(o_ref.dtype)


def _flash(q, k, v, batch, seq, *, tq, tk):
    hq, n, _ = q.shape
    hkv = k.shape[0]
    kv_shared = hkv * 2 == hq
    kvb = 1 if kv_shared else 2
    nq, nk = seq // tq, seq // tk

    def kv_map(b, hp, qi, ki):
        return (hp, b * nk + jnp.minimum(ki, (qi * tq + tq - 1) // tk), 0)

    return pl.pallas_call(
        functools.partial(_flash_kernel, tq=tq, tk=tk, kv_shared=kv_shared),
        grid=(batch, hq // 2, nq, nk),
        in_specs=[pl.BlockSpec((2, tq, LANES), lambda b, hp, qi, ki: (hp, b * nq + qi, 0)),
                  pl.BlockSpec((kvb, tk, LANES), kv_map),
                  pl.BlockSpec((kvb, tk, 64), kv_map)],
        out_specs=pl.BlockSpec((tq, LANES), lambda b, hp, qi, ki: (b * nq + qi, hp)),
        out_shape=jax.ShapeDtypeStruct((n, hq * 64), BF16),
        scratch_shapes=[pltpu.VMEM((2, tq, 1), F32), pltpu.VMEM((2, tq, 1), F32), pltpu.VMEM((2, tq, 64), F32)],
        compiler_params=pltpu.CompilerParams(
            dimension_semantics=("parallel", "parallel", "parallel", "arbitrary"), vmem_limit_bytes=VMEM_LIMIT),
        name="flash",
    )(q, k, v)


def _first_index_of_max(vals, lane):
    vmax = jnp.max(vals, axis=-1, keepdims=True)
    idx = jnp.min(jnp.where(vals == vmax, lane, 4 * LANES), axis=-1, keepdims=True)
    return vmax, idx


def _route(hm, wr_ref, br_ref):
    hh, hl = hm.astype(BF16), None
    hl = (hm - hh.astype(F32)).astype(BF16)
    logits = _dot(hh, wr_ref[0]) + _dot(hh, wr_ref[1]) + _dot(hl, wr_ref[0]) + br_ref[...]
    lane = _lane_iota(logits.shape)
    gl = jnp.where(lane < MOE_GROUPS, logits, -jnp.inf)
    gmax, grp = _first_index_of_max(gl, lane)
    p_grp = 1.0 / jnp.sum(jnp.exp(gl - gmax), axis=-1, keepdims=True)
    lo = MOE_GROUPS + MOE_PER_GROUP * grp
    sl = jnp.where(jnp.logical_and(lane >= lo, lane < lo + MOE_PER_GROUP), logits, -jnp.inf)
    v1, i1 = _first_index_of_max(sl, lane)
    sl2 = jnp.where(lane == i1, -jnp.inf, sl)
    v2, i2 = _first_index_of_max(sl2, lane)
    e2 = jnp.exp(v2 - v1)
    w1 = p_grp / (1.0 + e2)
    w2 = p_grp * e2 / (1.0 + e2)
    return jnp.where(lane == i1, w1, 0.0) + jnp.where(lane == i2, w2, 0.0)


def _out_moe_kernel(x_ref, mix_ref, wout_ref, gffn_ref, wr_ref, br_ref, w1_ref, w3_ref, w2_ref, o_ref,
                    xn_sc, hm_sc, gate_sc, acc_sc):
    e = pl.program_id(1)

    @pl.when(e == 0)
    def _():
        xn = x_ref[...] + _dot(mix_ref[...], wout_ref[...])
        xn_sc[...] = xn
        hm = _rms_rows(xn, gffn_ref[...])
        hm_sc[...] = hm.astype(BF16)
        gate_sc[...] = _route(hm, wr_ref, br_ref)
        acc_sc[...] = jnp.zeros_like(acc_sc)

    hb = hm_sc[...]
    gate = gate_sc[...]
    ge = jnp.sum(jnp.where(_lane_iota(gate.shape) == MOE_GROUPS + e, gate, 0.0), axis=-1, keepdims=True)
    h1 = _dot(hb, w1_ref[0])
    h3 = _dot(hb, w3_ref[0])
    a = (h1 * jax.nn.sigmoid(h1)) * h3 * ge
    acc_sc[...] += _dot(a.astype(BF16), w2_ref[0])

    @pl.when(e == MOE_EXPERTS - 1)
    def _():
        o_ref[...] = xn_sc[...] + acc_sc[...]


def _prep_moe(l, norm_ffn, moe_w_group, moe_b_group, moe_w_sub, moe_b_sub, moe_w1, moe_w3, moe_w2):
    wr = jnp.zeros((D_MODEL, LANES), F32)
    wr = wr.at[:, 0:MOE_GROUPS].set(moe_w_group[l])
    wr = wr.at[:, MOE_GROUPS:MOE_GROUPS + MOE_EXPERTS].set(
        jnp.transpose(moe_w_sub[l], (1, 0, 2)).reshape(D_MODEL, MOE_EXPERTS))
    wr_hi = wr.astype(BF16)
    wr_lo = (wr - wr_hi.astype(F32)).astype(BF16)
    br = jnp.zeros((1, LANES), F32)
    br = br.at[0, 0:MOE_GROUPS].set(moe_b_group[l])
    br = br.at[0, MOE_GROUPS:MOE_GROUPS + MOE_EXPERTS].set(moe_b_sub[l].reshape(-1))
    return dict(gffn=norm_ffn[l][None, :], wr=jnp.stack([wr_hi, wr_lo]), br=br,
                w1=moe_w1[l].astype(BF16), w3=moe_w3[l].astype(BF16), w2=moe_w2[l].astype(BF16))


def _out_moe(x2d, mix, w_out, prm, tm):
    n = x2d.shape[0]
    row = lambda w: pl.BlockSpec((tm, w), lambda i, e: (i, 0))
    cst = lambda a: pl.BlockSpec(a.shape, lambda i, e: (0,) * a.ndim)
    return pl.pallas_call(
        _out_moe_kernel,
        grid=(n // tm, MOE_EXPERTS),
        in_specs=[row(D_MODEL), row(mix.shape[1]), cst(w_out), cst(prm["gffn"]), cst(prm["wr"]), cst(prm["br"]),
                  pl.BlockSpec((1, D_MODEL, MOE_FF), lambda i, e: (e, 0, 0)),
                  pl.BlockSpec((1, D_MODEL, MOE_FF), lambda i, e: (e, 0, 0)),
                  pl.BlockSpec((1, MOE_FF, D_MODEL), lambda i, e: (e, 0, 0))],
        out_specs=row(D_MODEL),
        out_shape=jax.ShapeDtypeStruct((n, D_MODEL), F32),
        scratch_shapes=[pltpu.VMEM((tm, D_MODEL), F32), pltpu.VMEM((tm, D_MODEL), BF16),
                        pltpu.VMEM((tm, LANES), F32), pltpu.VMEM((tm, D_MODEL), F32)],
        compiler_params=pltpu.CompilerParams(dimension_semantics=("parallel", "arbitrary"),
                                             vmem_limit_bytes=VMEM_LIMIT),
        name="out_moe",
    )(x2d, mix, w_out, prm["gffn"], prm["wr"], prm["br"], prm["w1"], prm["w3"], prm["w2"])


def _proj_b_kernel(x_ref, vec_ref, win_ref, gmc_ref, wmix_ref, bmix_ref,
                   q_o, k_o, v_o, gv_o, gm_o, kmean_o, ka_o, va_o, *, tiles_per_seq):
    i = pl.program_id(0)
    x = x_ref[...]
    h = _rms_rows(x, vec_ref[0:1, :]).astype(BF16)
    z = _dot(h, win_ref[...])
    gmc = gmc_ref[...]
    q = z[:, 0:512]
    q = q * lax.rsqrt(_group_sumsq(q, gmc) * (1.0 / HEAD_DIM) + EPS) * vec_ref[1:2, 0:512]
    q_o[...] = q
    k = z[:, 512:768]
    k = k * lax.rsqrt(_group_sumsq(k, gmc) * (1.0 / HEAD_DIM) + EPS) * vec_ref[1:2, 512:768]
    k_o[...] = k
    v = z[:, 768:1024]
    v_o[...] = v
    kmean_o[0] = jnp.mean(k, axis=0, keepdims=True)
    blk = i % tiles_per_seq
    onehot = jnp.where(_lane_iota((ROW_TILE, LANES)) == 64 + blk, 1.0, 0.0)
    for hk in range(MOBA_KV_HEADS):
        ka_o[hk] = (_head_block(k, hk, HEAD_DIM) + onehot).astype(BF16)
        va_o[hk] = v[:, 64 * hk:64 * (hk + 1)].astype(BF16)

    zz = z[:, 1024:2048]
    zz = zz * (0.5 * (1.0 + jnp.tanh(math.sqrt(2.0 / math.pi) * (zz + 0.044715 * (zz * zz * zz)))))
    u = zz[:, 0:GM_WIDTH]
    g = zz[:, GM_WIDTH:]
    gc = g - jnp.mean(g, axis=-1, keepdims=True)
    gv = gc * lax.rsqrt(jnp.mean(gc * gc, axis=-1, keepdims=True) + EPS) * vec_ref[2:3, 0:512] + vec_ref[2:3, 512:1024]
    gv_o[...] = gv
    gvb = gv.astype(BF16)
    for gg in range(GM_GROUPS):
        sl = slice(GM_GROUP_DIM * gg, GM_GROUP_DIM * (gg + 1))
        mixg = _dot(wmix_ref[gg], gvb[:, sl]) + bmix_ref[gg]
        gm_o[:, sl] = (u[:, sl] * mixg).astype(BF16)


def _prep_b(i, norm_mix_l, w_in_b, moba_g_q, moba_g_k, gm_ln_g, gm_ln_b):
    row1 = jnp.concatenate([jnp.tile(moba_g_q[i], MOBA_HEADS), jnp.tile(moba_g_k[i], MOBA_KV_HEADS),
                            jnp.zeros((256,), F32)])
    row2 = jnp.concatenate([gm_ln_g[i], gm_ln_b[i]])
    g64 = np.arange(256) // 64
    gmc = (g64[:, None] == g64[None, :]).astype(np.float32)
    return dict(vec=jnp.stack([norm_mix_l, row1, row2]), win=w_in_b[i].astype(BF16), gmc=jnp.asarray(gmc, BF16))


def _gmlp_mix_weights(w_s, b_s, pos, period):
    n = pos.shape[0]
    cp, ch, seq = pos % GM_CHUNK, pos // GM_CHUNK, np.arange(n) // period
    mask = (ch[:, None] == ch[None, :]) & (pos[None, :] <= pos[:, None]) & (seq[:, None] == seq[None, :])
    c0 = cp[:period]
    assert all((cp[i * period:(i + 1) * period] == c0).all() for i in range(n // period))
    if (np.diff(c0) == 1).all():
        w0 = w_s[:, c0[0]:c0[0] + period, c0[0]:c0[0] + period]
        b0 = b_s[:, c0[0]:c0[0] + period]
    else:
        w0 = w_s[:, c0[:, None], c0[None, :]]
        b0 = b_s[:, c0]
    w = jnp.where(mask[None], jnp.tile(w0, (1, n // period, n // period)), 0.0)
    b = jnp.broadcast_to(jnp.tile(b0, (1, n // period))[:, :, None], (GM_GROUPS, n, GM_GROUP_DIM))
    return w, b


def _proj_b(x2d, prm, wmix, bmix, tiles_per_seq):
    n = x2d.shape[0]
    nt = n // ROW_TILE
    row = lambda w: pl.BlockSpec((ROW_TILE, w), lambda i: (i, 0))
    hm = lambda nh, w: pl.BlockSpec((nh, ROW_TILE, w), lambda i: (0, i, 0))
    consts = [prm["vec"], prm["win"], prm["gmc"], wmix, bmix]
    sds = jax.ShapeDtypeStruct
    return pl.pallas_call(
        functools.partial(_proj_b_kernel, tiles_per_seq=tiles_per_seq),
        grid=(nt,),
        in_specs=[row(D_MODEL)] + [_const_spec(a) for a in consts],
        out_specs=[row(512), row(256), row(256), row(GM_WIDTH), row(GM_WIDTH),
                   pl.BlockSpec((1, 1, 256), lambda i: (i, 0, 0)), hm(MOBA_KV_HEADS, LANES), hm(MOBA_KV_HEADS, 64)],
        out_shape=[sds((n, 512), F32), sds((n, 256), F32), sds((n, 256), F32), sds((n, GM_WIDTH), F32),
                   sds((n, GM_WIDTH), BF16), sds((nt, 1, 256), F32),
                   sds((MOBA_KV_HEADS, n, LANES), BF16), sds((MOBA_KV_HEADS, n, 64), BF16)],
        compiler_params=pltpu.CompilerParams(dimension_semantics=("parallel",), vmem_limit_bytes=VMEM_LIMIT),
        name="proj_b",
    )(x2d, *consts)


def _moba_gate_kernel(q_ref, km_ref, qa_o, *, tiles_per_seq):
    cur = pl.program_id(0) % tiles_per_seq
    q = q_ref[...]
    qh = q.astype(BF16)
    ql = (q - qh.astype(F32)).astype(BF16)
    gate = _dot(qh, km_ref[0, 0]) + _dot(qh, km_ref[0, 1]) + _dot(ql, km_ref[0, 0])
    lane = _lane_iota(gate.shape)
    n = lane % MOBA_MAX_BLOCKS
    valid = n < cur
    g = jnp.where(valid, gate, -jnp.inf)
    rank = jnp.zeros(gate.shape, jnp.int32)
    for j in range(1, MOBA_MAX_BLOCKS):
        wrapped = n + j >= MOBA_MAX_BLOCKS
        pv = jnp.where(wrapped, pltpu.roll(g, MOBA_MAX_BLOCKS - j, axis=1), pltpu.roll(g, LANES - j, axis=1))
        beats = jnp.logical_or(pv > g, jnp.logical_and(pv == g, wrapped))
        rank = rank + beats.astype(jnp.int32)
    sel = jnp.logical_or(jnp.logical_and(valid, rank < MOBA_TOPK), n == cur)
    sb = jnp.where(sel, 0.0, NEG_BIG)
    qs = q * ATTN_SCALE
    for hh in range(MOBA_HEADS):
        shift = (64 - MOBA_MAX_BLOCKS * hh) % LANES
        sbh = pltpu.roll(sb, shift, axis=1) if shift else sb
        sbh = jnp.where(jnp.logical_and(lane >= 64, lane < 64 + MOBA_MAX_BLOCKS), sbh, 0.0)
        qa_o[hh] = (_head_block(qs, hh, HEAD_DIM) + sbh).astype(BF16)


def _moba_km(kmean, batch, nb):
    km = kmean.reshape(batch, nb, MOBA_KV_HEADS, HEAD_DIM)
    km = jnp.repeat(km, MOBA_HEADS // MOBA_KV_HEADS, axis=2)
    km = jnp.pad(jnp.transpose(km, (0, 2, 3, 1)), ((0, 0), (0, 0), (0, 0), (0, MOBA_MAX_BLOCKS - nb)))
    km = jnp.einsum("bhdn,hg->bhdgn", km, jnp.eye(MOBA_HEADS, dtype=F32)).reshape(batch, 512, LANES)
    hi = km.astype(BF16)
    lo = (km - hi.astype(F32)).astype(BF16)
    return jnp.stack([hi, lo], axis=1)


def _moba_gate(q, km, tiles_per_seq):
    n = q.shape[0]
    return pl.pallas_call(
        functools.partial(_moba_gate_kernel, tiles_per_seq=tiles_per_seq),
        grid=(n // ROW_TILE,),
        in_specs=[pl.BlockSpec((ROW_TILE, 512), lambda i: (i, 0)),
                  pl.BlockSpec((1, 2, 512, LANES), lambda i: (i // tiles_per_seq, 0, 0, 0))],
        out_specs=pl.BlockSpec((MOBA_HEADS, ROW_TILE, LANES), lambda i: (0, i, 0)),
        out_shape=jax.ShapeDtypeStruct((MOBA_HEADS, n, LANES), BF16),
        compiler_params=pltpu.CompilerParams(dimension_semantics=("parallel",), vmem_limit_bytes=VMEM_LIMIT),
        name="moba_gate",
    )(q, km)


PAGE_CHUNK = 16


def _page_specs(rows, width, pps, layer):
    return [pl.BlockSpec((None, None, rows, width),
                         functools.partial(lambda s, j, pt, p: (layer, pt[s, j * pps + p], 0, 0), p=p))
            for p in range(pps)]


def _cat(pages, lo, hi, axis):
    parts = [pages[p][...] for p in range(lo, hi)]
    return parts[0] if len(parts) == 1 else jnp.concatenate(parts, axis=axis)


def _softmax_step(s, pv, m, l, acc):
    m_new = jnp.maximum(m, jnp.max(s, axis=-1, keepdims=True))
    alpha = jnp.exp(m - m_new)
    p = jnp.exp(s - m_new)
    return m_new, alpha * l + jnp.sum(p, axis=-1, keepdims=True), alpha * acc + pv(p.astype(BF16))


def _new_row_mask(shape, dec_seq):
    qidx = lax.broadcasted_iota(jnp.int32, shape, 0) % dec_seq
    return lax.broadcasted_iota(jnp.int32, shape, 1) <= qidx


def _mla_dec_kernel(pt_ref, *refs, pps, dec_seq):
    ckv_pages, kr_pages = refs[0:pps], refs[pps:2 * pps]
    (q_ref, cnew_ref, krnew_ref, wukg_ref, wukt_ref, wuv_ref, g_ref, o_ref,
     qp_sc, qr_sc, m_sc, l_sc, acc_sc) = refs[2 * pps:]
    j = pl.program_id(1)
    rows = MLA_HEADS * dec_seq

    @pl.when(j == 0)
    def _():
        for hh in range(MLA_HEADS):
            qh = q_ref[hh]
            qp_sc[dec_seq * hh:dec_seq * (hh + 1), :] = _dot(qh[:, 0:64].astype(BF16), wukg_ref[hh])
            qr_sc[dec_seq * hh:dec_seq * (hh + 1), :] = qh[:, 64:96]
        m_sc[...] = jnp.full_like(m_sc, M_INIT)
        l_sc[...] = jnp.zeros_like(l_sc)
        acc_sc[...] = jnp.zeros_like(acc_sc)

    lhs = jnp.concatenate([qp_sc[...].astype(BF16), wukt_ref[...]], axis=0)
    qr = qr_sc[...].astype(BF16)
    g = g_ref[...]

    def chunk(c, krt, carry, masked):
        keys = c.shape[0]
        cn = (c * lax.rsqrt(jnp.mean(c * c, axis=-1, keepdims=True) + EPS) * g).astype(BF16)
        big = _dot_nt(lhs, cn)
        kexp = big[rows:, :]
        ss = jnp.sum((kexp * kexp).reshape(MLA_HEADS, MLA_NOPE_DIM, keys), axis=1)
        r = lax.rsqrt(ss * (1.0 / MLA_NOPE_DIM) + EPS)
        s = big[0:rows, :].reshape(MLA_HEADS, dec_seq, keys) * r[:, None, :]
        s = s.reshape(rows, keys) + _dot(qr, krt.astype(BF16))
        if masked:
            s = jnp.where(_new_row_mask(s.shape, dec_seq), s, -jnp.inf)
        return _softmax_step(s, lambda p: _dot(p, cn), *carry)

    carry = (m_sc[...], l_sc[...], acc_sc[...])
    for p0 in range(0, pps, PAGE_CHUNK):
        p1 = min(p0 + PAGE_CHUNK, pps)
        carry = chunk(_cat(ckv_pages, p0, p1, 0), _cat(kr_pages, p0, p1, 1), carry, False)
    m_sc[...], l_sc[...], acc_sc[...] = carry

    @pl.when(j == pl.num_programs(1) - 1)
    def _():
        m, l, acc = chunk(cnew_ref[0], krnew_ref[0], (m_sc[...], l_sc[...], acc_sc[...]), True)
        lat = acc / l
        for hh in range(MLA_HEADS):
            lat_h = lat[dec_seq * hh:dec_seq * (hh + 1), :].astype(BF16)
            o_ref[0, :, 64 * hh:64 * (hh + 1)] = _dot(lat_h, wuv_ref[hh])


def _mla_decode(cache_ckv, cache_kr, layer, page_table, qm, cnew, krnew, wukg, wukt, wuv, g_ckv, pps, dec_seq):
    n_seq, n_pages = page_table.shape
    rows = MLA_HEADS * dec_seq
    per_seq = lambda shp: pl.BlockSpec((1,) + shp, lambda s, j, pt: (s,) + (0,) * len(shp))
    cst = lambda a: pl.BlockSpec(a.shape, lambda s, j, pt: (0,) * a.ndim)
    grid_spec = pltpu.PrefetchScalarGridSpec(
        num_scalar_prefetch=1, grid=(n_seq, n_pages // pps),
        in_specs=(_page_specs(LANES, MLA_KV_RANK, pps, layer) + _page_specs(MLA_ROPE_DIM, LANES, pps, layer)
                  + [pl.BlockSpec((MLA_HEADS, dec_seq, LANES), lambda s, j, pt: (0, s, 0)),
                     per_seq((LANES, MLA_KV_RANK)), per_seq((MLA_ROPE_DIM, LANES)),
                     cst(wukg), cst(wukt), cst(wuv), cst(g_ckv)]),
        out_specs=per_seq((dec_seq, MLA_HEADS * MLA_V_DIM)),
        scratch_shapes=[pltpu.VMEM((rows, MLA_KV_RANK), F32), pltpu.VMEM((rows, MLA_ROPE_DIM), F32),
                        pltpu.VMEM((rows, 1), F32), pltpu.VMEM((rows, 1), F32), pltpu.VMEM((rows, MLA_KV_RANK), F32)])
    return pl.pallas_call(
        functools.partial(_mla_dec_kernel, pps=pps, dec_seq=dec_seq),
        grid_spec=grid_spec,
        out_shape=jax.ShapeDtypeStruct((n_seq, dec_seq, MLA_HEADS * MLA_V_DIM), F32),
        compiler_params=pltpu.CompilerParams(dimension_semantics=("parallel", "arbitrary"),
                                             vmem_limit_bytes=VMEM_LIMIT),
        name="mla_decode",
    )(page_table, *([cache_ckv] * pps), *([cache_kr] * pps), qm, cnew, krnew, wukg, wukt, wuv, g_ckv)


def _fox_dec_kernel(pt_ref, *refs, pps, dec_seq):
    k_pages, v_pages, lf_pages = refs[0:pps], refs[pps:2 * pps], refs[2 * pps:3 * pps]
    (qbd_ref, knew_ref, vnew_ref, lfnew_ref, utri_ref, o_ref, m_sc, l_sc, acc_sc, base_sc) = refs[3 * pps:]
    j = pl.program_id(1)
    rows = FOX_HEADS * dec_seq

    @pl.when(j == 0)
    def _():
        m_sc[...] = jnp.full_like(m_sc, M_INIT)
        l_sc[...] = jnp.zeros_like(l_sc)
        acc_sc[...] = jnp.zeros_like(acc_sc)
        base_sc[...] = jnp.zeros_like(base_sc)

    qbd = qbd_ref[0]
    utri = utri_ref[...]

    def cum_lanes(lf):
        a, b, c = _split3(lf)
        return _dot(a, utri) + _dot(b, utri) + _dot(c, utri)

    def chunk(kt, vt, c, carry, masked):
        keys = kt.shape[1]
        s = _dot(qbd, kt.astype(BF16)).reshape(FOX_HEADS, dec_seq, keys) - c[:, None, :]
        s = s.reshape(rows, keys)
        if masked:
            s = jnp.where(_new_row_mask(s.shape, dec_seq), s, -jnp.inf)
        vtb = vt.astype(BF16)
        return _softmax_step(s, lambda p: _dot_nt(p, vtb), *carry)

    cum = cum_lanes(_cat(lf_pages, 0, pps, 0))
    base = base_sc[...]
    c_pages = []
    for p in range(pps):
        c_loc = cum[FOX_HEADS * p:FOX_HEADS * (p + 1), :]
        c_pages.append(c_loc + base)
        base = base + c_loc[:, LANES - 1:LANES]
    base_sc[...] = base
    carry = (m_sc[...], l_sc[...], acc_sc[...])
    for p0 in range(0, pps, PAGE_CHUNK):
        p1 = min(p0 + PAGE_CHUNK, pps)
        c = c_pages[p0] if p1 - p0 == 1 else jnp.concatenate(c_pages[p0:p1], axis=1)
        carry = chunk(_cat(k_pages, p0, p1, 1), _cat(v_pages, p0, p1, 1), c, carry, False)
    m_sc[...], l_sc[...], acc_sc[...] = carry

    @pl.when(j == pl.num_programs(1) - 1)
    def _():
        c_new = cum_lanes(lfnew_ref[0]) + base_sc[...]
        m, l, acc = chunk(knew_ref[0], vnew_ref[0], c_new, (m_sc[...], l_sc[...], acc_sc[...]), True)
        o = acc / l
        for hq in range(FOX_HEADS):
            hk = hq // 2
            o_ref[0, :, 64 * hq:64 * (hq + 1)] = o[dec_seq * hq:dec_seq * (hq + 1), 64 * hk:64 * (hk + 1)]


def _fox_decode(cache_k, cache_v, cache_lft, layer, page_table, qbd, knew, vnew, lfnew, pps, dec_seq):
    n_seq, n_pages = page_table.shape
    rows = FOX_HEADS * dec_seq
    per_seq = lambda shp: pl.BlockSpec((1,) + shp, lambda s, j, pt: (s,) + (0,) * len(shp))
    utri = jnp.asarray(np.triu(np.ones((LANES, LANES), np.float32)), BF16)
    lf_specs = [pl.BlockSpec((None, None, FOX_HEADS, LANES),
                             functools.partial(lambda s, j, pt, p: (layer, pt[s, j * pps + p], 0, 0), p=p))
                for p in range(pps)]
    grid_spec = pltpu.PrefetchScalarGridSpec(
        num_scalar_prefetch=1, grid=(n_seq, n_pages // pps),
        in_specs=(_page_specs(256, LANES, pps, layer) + _page_specs(256, LANES, pps, layer) + lf_specs
                  + [per_seq((rows, 256)), per_seq((256, LANES)), per_seq((256, LANES)), per_seq((FOX_HEADS, LANES)),
                     pl.BlockSpec((LANES, LANES), lambda s, j, pt: (0, 0))]),
        out_specs=per_seq((dec_seq, FOX_HEADS * HEAD_DIM)),
        scratch_shapes=[pltpu.VMEM((rows, 1), F32), pltpu.VMEM((rows, 1), F32), pltpu.VMEM((rows, 256), F32),
                        pltpu.VMEM((FOX_HEADS, 1), F32)])
    return pl.pallas_call(
        functools.partial(_fox_dec_kernel, pps=pps, dec_seq=dec_seq),
        grid_spec=grid_spec,
        out_shape=jax.ShapeDtypeStruct((n_seq, dec_seq, FOX_HEADS * HEAD_DIM), F32),
        compiler_params=pltpu.CompilerParams(dimension_semantics=("parallel", "arbitrary"),
                                             vmem_limit_bytes=VMEM_LIMIT),
        name="fox_decode",
    )(page_table, *([cache_k] * pps), *([cache_v] * pps), *([cache_lft] * pps), qbd, knew, vnew, lfnew, utri)


def _moba_dec_kernel(pt_ref, *refs, pps, dec_seq, nb):
    k_pages, v_pages = refs[0:pps], refs[pps:2 * pps]
    (qbd_ref, qg_ref, knew_ref, vnew_ref, o_ref, m_sc, l_sc, kmt_sc, o_sc) = refs[2 * pps:]
    j = pl.program_id(1)
    rows = MOBA_HEADS * dec_seq
    bps = pps // 2
    qbd = qbd_ref[0]
    lane = _lane_iota((rows, LANES))
    lane_k = _lane_iota((256, LANES))

    @pl.when(j == 0)
    def _():
        m_sc[...] = jnp.zeros_like(m_sc)
        l_sc[...] = jnp.zeros_like(l_sc)
        kmt_sc[...] = jnp.zeros_like(kmt_sc)

    m_all, l_all, kmt = m_sc[...], l_sc[...], kmt_sc[...]
    for b in range(bps):
        n = j * bps + b
        kt = _cat(k_pages, 2 * b, 2 * b + 2, 1)
        vt = _cat(v_pages, 2 * b, 2 * b + 2, 1)
        s = _dot(qbd, kt.astype(BF16))
        mn = jnp.max(s, axis=-1, keepdims=True)
        p = jnp.exp(s - mn)
        o_sc[n] = _dot_nt(p.astype(BF16), vt.astype(BF16))
        here = lane == n
        m_all = jnp.where(here, mn, m_all)
        l_all = jnp.where(here, jnp.sum(p, axis=-1, keepdims=True), l_all)
        kmt = jnp.where(lane_k == n, jnp.sum(kt, axis=-1, keepdims=True) * (1.0 / MOBA_BLOCK), kmt)
    m_sc[...], l_sc[...], kmt_sc[...] = m_all, l_all, kmt

    @pl.when(j == pl.num_programs(1) - 1)
    def _():
        s = _dot(qbd, knew_ref[0].astype(BF16))
        s = jnp.where(_new_row_mask(s.shape, dec_seq), s, -jnp.inf)
        m_o = jnp.max(s, axis=-1, keepdims=True)
        p = jnp.exp(s - m_o)
        l_o = jnp.sum(p, axis=-1, keepdims=True)
        o_o = _dot_nt(p.astype(BF16), vnew_ref[0].astype(BF16))
        qg = qg_ref[0]
        qh = qg.astype(BF16)
        ql = (qg - qh.astype(F32)).astype(BF16)
        kh = kmt.astype(BF16)
        kl = (kmt - kh.astype(F32)).astype(BF16)
        g_all = _dot(qh, kh) + _dot(qh, kl) + _dot(ql, kh)
        g = jnp.where(lane < nb, g_all, -jnp.inf)
        sel = jnp.zeros(g.shape, jnp.bool_)
        for _ in range(min(MOBA_TOPK, nb)):
            vmax, idx = _first_index_of_max(g, lane)
            hit = lane == idx
            sel = jnp.logical_or(sel, jnp.logical_and(hit, vmax > -jnp.inf))
            g = jnp.where(hit, -jnp.inf, g)
        m_top = jnp.maximum(jnp.max(jnp.where(sel, m_all, -jnp.inf), axis=-1, keepdims=True), m_o)
        w = jnp.where(sel, jnp.exp(m_all - m_top), 0.0)
        w_o = jnp.exp(m_o - m_top)
        den = jnp.sum(w * l_all, axis=-1, keepdims=True) + w_o * l_o

        def body(nn, acc):
            col = jnp.sum(jnp.where(lane == nn, w, 0.0), axis=-1, keepdims=True)
            return acc + col * o_sc[nn]

        o = lax.fori_loop(0, nb, body, w_o * o_o) / den
        for hq in range(MOBA_HEADS):
            hk = hq // 2
            o_ref[0, :, 64 * hq:64 * (hq + 1)] = o[dec_seq * hq:dec_seq * (hq + 1), 64 * hk:64 * (hk + 1)]


def _moba_decode(cache_k, cache_v, layer, page_table, qbd, qg, knew, vnew, pps, dec_seq):
    n_seq, n_pages = page_table.shape
    rows = MOBA_HEADS * dec_seq
    nb = n_pages // 2
    per_seq = lambda shp: pl.BlockSpec((1,) + shp, lambda s, j, pt: (s,) + (0,) * len(shp))
    grid_spec = pltpu.PrefetchScalarGridSpec(
        num_scalar_prefetch=1, grid=(n_seq, n_pages // pps),
        in_specs=(_page_specs(256, LANES, pps, layer) + _page_specs(256, LANES, pps, layer)
                  + [per_seq((rows, 256)), per_seq((rows, 256)), per_seq((256, LANES)), per_seq((256, LANES))]),
        out_specs=per_seq((dec_seq, MOBA_HEADS * HEAD_DIM)),
        scratch_shapes=[pltpu.VMEM((rows, LANES), F32), pltpu.VMEM((rows, LANES), F32), pltpu.VMEM((256, LANES), F32),
                        pltpu.VMEM((nb, rows, 256), F32)])
    return pl.pallas_call(
        functools.partial(_moba_dec_kernel, pps=pps, dec_seq=dec_seq, nb=nb),
        grid_spec=grid_spec,
        out_shape=jax.ShapeDtypeStruct((n_seq, dec_seq, MOBA_HEADS * HEAD_DIM), F32),
        compiler_params=pltpu.CompilerParams(dimension_semantics=("parallel", "arbitrary"),
                                             vmem_limit_bytes=VMEM_LIMIT),
        name="moba_decode",
    )(page_table, *([cache_k] * pps), *([cache_v] * pps), qbd, qg, knew, vnew)


def _block_diag_queries(q4):
    nh = q4.shape[0]
    place = jnp.asarray(np.arange(nh)[:, None] // 2 == np.arange(nh // 2)[None, :], q4.dtype)
    out = jnp.einsum("hsqd,hk->shqkd", q4, place)
    return out.reshape(q4.shape[1], nh * q4.shape[2], (nh // 2) * 64)


def _pad_rows(a, rows):
    return jnp.pad(a, ((0, 0), (0, rows - a.shape[1]), (0, 0)))


def kernel(x_prompt, x_sample, cache_mla_ckv, cache_mla_krope, cache_fox_k, cache_fox_v, cache_fox_logf, cache_moba_k, cache_moba_v, page_table, norm_mix, norm_ffn, w_in_a, mla_g_cq, mla_w_uq, mla_g_ckv, mla_w_ukv, mla_g_qn, mla_g_qr, mla_g_kn, mla_g_kr, fox_g_q, fox_g_k, fox_b_f, w_out_a, w_in_b, moba_g_q, moba_g_k, gm_ln_g, gm_ln_b, gm_w_s, gm_b_s, w_out_b, moe_w_group, moe_b_group, moe_w_sub, moe_b_sub, moe_w1, moe_w3, moe_w2):
    B, T, _ = x_prompt.shape
    S, DS, _ = x_sample.shape
    n_pages = page_table.shape[1]
    page = cache_mla_ckv.shape[2]
    past = n_pages * page
    n_phys = cache_mla_ckv.shape[1]
    assert page == LANES and T % 1024 == 0 and T // MOBA_BLOCK <= MOBA_MAX_BLOCKS
    assert (S * DS) % ROW_TILE == 0 and ROW_TILE % DS == 0 and DS <= LANES
    assert past % MOBA_BLOCK == 0 and past % GM_CHUNK == 0 and n_pages % 2 == 0
    pps = math.gcd(16, n_pages)
    tps = T // ROW_TILE
    np_rows, ns_rows = B * T, S * DS
    xp = x_prompt.reshape(np_rows, D_MODEL)
    xs = x_sample.reshape(ns_rows, D_MODEL)
    pos_s = past + np.arange(ROW_TILE) % DS

    pa = _prep_a(0, norm_mix[0], w_in_a, mla_g_cq, mla_w_uq, mla_g_ckv, mla_w_ukv, mla_g_qn, mla_g_qr, mla_g_kn,
                 mla_g_kr, fox_g_q, fox_g_k, fox_b_f)
    p_ckv, p_kr, p_fk, p_fv, p_fl, qm, km, vm, qf, kf, vf = _proj_a(xp, pa, _rot_tables(jnp.arange(T)), tps)
    s_ckv, s_kr, s_fk, s_fv, s_fl, qm_s, _, _, qf_s, _, _ = _proj_a(xs, pa, _rot_tables(jnp.asarray(pos_s)), 1)
    mix_p = jnp.concatenate([_flash(qm, km, vm, B, T, tq=1024, tk=1024),
                             _flash(qf, kf, vf, B, T, tq=1024, tk=1024)], axis=1)

    wukv = mla_w_ukv[0].reshape(MLA_KV_RANK, MLA_HEADS, 128)
    wuk_t = jnp.transpose(wukv[:, :, 0:64], (1, 2, 0))
    wukg = (wuk_t * mla_g_kn[0][None, :, None]).astype(BF16)
    wukt = wuk_t.reshape(MLA_HEADS * 64, MLA_KV_RANK).astype(BF16)
    wuv = jnp.transpose(wukv[:, :, 64:128], (1, 0, 2)).astype(BF16)
    rows_last = lambda c: jnp.moveaxis(c, 2, -1).reshape(c.shape[0], n_phys, -1, page)
    new_t = lambda a: jnp.pad(jnp.transpose(a.reshape(S, DS, -1), (0, 2, 1)), ((0, 0), (0, 0), (0, LANES - DS)))
    o_mla = _mla_decode(cache_mla_ckv, rows_last(cache_mla_krope), 0, page_table, qm_s.astype(F32),
                        _pad_rows(s_ckv.reshape(S, DS, -1), LANES), new_t(s_kr),
                        wukg, wukt, wuv, mla_g_ckv[0][None, :], pps, DS)
    qbd_f = _block_diag_queries(qf_s[:, :, 0:64].reshape(FOX_HEADS, S, DS, 64))
    o_fox = _fox_decode(rows_last(cache_fox_k), rows_last(cache_fox_v), rows_last(cache_fox_logf),
                        0, page_table, qbd_f, new_t(s_fk), new_t(s_fv), new_t(s_fl), pps, DS)
    mix_s = jnp.concatenate([o_mla.reshape(ns_rows, -1), o_fox.reshape(ns_rows, -1)], axis=1).astype(BF16)

    pm = _prep_moe(0, norm_ffn, moe_w_group, moe_b_group, moe_w_sub, moe_b_sub, moe_w1, moe_w3, moe_w2)
    w_out = w_out_a[0].astype(BF16)
    xp = _out_moe(xp, mix_p, w_out, pm, min(1024, np_rows))
    xs = _out_moe(xs, mix_s, w_out, pm, min(1024, ns_rows))

    pb = _prep_b(0, norm_mix[1], w_in_b, moba_g_q, moba_g_k, gm_ln_g, gm_ln_b)
    wmix_p, bmix_p = _gmlp_mix_weights(gm_w_s[0], gm_b_s[0], np.arange(ROW_TILE), GM_CHUNK)
    wmix_s, bmix_s = _gmlp_mix_weights(gm_w_s[0], gm_b_s[0], pos_s, DS)
    q_p, p_mk, p_mv, _, gm_p, kmean, ka, va = _proj_b(xp, pb, wmix_p.astype(BF16), bmix_p, tps)
    q_s, s_mk, s_mv, s_gv, gm_s, _, _, _ = _proj_b(xs, pb, wmix_s.astype(BF16), bmix_s, 1)
    qa = _moba_gate(q_p, _moba_km(kmean, B, tps), tps)
    mix_p = jnp.concatenate([_flash(qa, ka, va, B, T, tq=1024, tk=1024), gm_p], axis=1)

    q4 = jnp.transpose(q_s.reshape(S, DS, MOBA_HEADS, 64), (2, 0, 1, 3))
    o_moba = _moba_decode(rows_last(cache_moba_k), rows_last(cache_moba_v), 0,
                          page_table, _block_diag_queries((q4 * ATTN_SCALE).astype(BF16)), _block_diag_queries(q4),
                          new_t(s_mk), new_t(s_mv), pps, DS)
    mix_s = jnp.concatenate([o_moba.reshape(ns_rows, -1).astype(BF16), gm_s], axis=1)

    pm = _prep_moe(1, norm_ffn, moe_w_group, moe_b_group, moe_w_sub, moe_b_sub, moe_w1, moe_w3, moe_w2)
    w_out = w_out_b[0].astype(BF16)
    xp = _out_moe(xp, mix_p, w_out, pm, min(1024, np_rows))
    xs = _out_moe(xs, mix_s, w_out, pm, min(1024, ns_rows))

    pr = lambda a, *tail: a.reshape((1, B, T) + tail)
    sr = lambda a, *tail: a.reshape((1, S, DS) + tail)
    return (xp.reshape(B, T, D_MODEL), xs.reshape(S, DS, D_MODEL),
            pr(p_ckv, MLA_KV_RANK), pr(p_kr, MLA_ROPE_DIM), pr(p_fk, FOX_KV_HEADS, 64), pr(p_fv, FOX_KV_HEADS, 64),
            pr(p_fl, FOX_HEADS), pr(p_mk, MOBA_KV_HEADS, 64), pr(p_mv, MOBA_KV_HEADS, 64),
            sr(s_ckv, MLA_KV_RANK), sr(s_kr, MLA_ROPE_DIM), sr(s_fk, FOX_KV_HEADS, 64), sr(s_fv, FOX_KV_HEADS, 64),
            sr(s_fl, FOX_HEADS), sr(s_mk, MOBA_KV_HEADS, 64), sr(s_mv, MOBA_KV_HEADS, 64), sr(s_gv, GM_WIDTH))
```

```python
import functools
import math

import jax
import jax.numpy as jnp
import numpy as np
from jax import lax
from jax.experimental import pallas as pl
from jax.experimental.pallas import tpu as pltpu

F32 = jnp.float32
BF16 = jnp.bfloat16

D_MODEL = 1024
HEAD_DIM = 64
MLA_HEADS = 8
MLA_Q_RANK = 256
MLA_KV_RANK = 256
MLA_NOPE_DIM = 64
MLA_ROPE_DIM = 32
MLA_V_DIM = 64
ROPE_THETA = 10000.0
FOX_HEADS = 8
FOX_KV_HEADS = 4
MOBA_HEADS = 8
MOBA_KV_HEADS = 4
MOBA_BLOCK = 256
MOBA_TOPK = 3
GM_GROUPS = 4
GM_GROUP_DIM = 128
GM_WIDTH = GM_GROUPS * GM_GROUP_DIM
GM_CHUNK = 128
MOE_GROUPS = 4
MOE_PER_GROUP = 4
MOE_EXPERTS = 16
MOE_FF = 256
EPS = 1e-6
MLA_SCALE = (MLA_NOPE_DIM + MLA_ROPE_DIM) ** -0.5
ATTN_SCALE = HEAD_DIM ** -0.5

LANES = 128
ROW_TILE = 256
MOBA_MAX_BLOCKS = 16
NEG_BIG = -(2.0 ** 100)
M_INIT = -1e30
VMEM_LIMIT = 48 * 1024 * 1024


def _dot(a, b):
    return jnp.dot(a, b, preferred_element_type=F32)


def _dot_nt(a, b):
    return lax.dot_general(a, b, (((1,), (1,)), ((), ())), preferred_element_type=F32)


def _split3(x):
    hi = x.astype(BF16)
    r = x - hi.astype(F32)
    mid = r.astype(BF16)
    lo = (r - mid.astype(F32)).astype(BF16)
    return hi, mid, lo


def _lane_iota(shape):
    return lax.broadcasted_iota(jnp.int32, shape, len(shape) - 1)


def _head_block(x, h, width):
    start = h * width
    blk = x[:, (start // LANES) * LANES:(start // LANES + 1) * LANES]
    off = start % LANES
    if off:
        blk = pltpu.roll(blk, LANES - off, axis=1)
    return jnp.where(_lane_iota(blk.shape) < width, blk, 0.0)


def _group_sumsq(x, gmat):
    outs = []
    for j in range(x.shape[1] // 256):
        xs = x[:, 256 * j:256 * (j + 1)]
        outs.append(_dot((xs * xs).astype(BF16), gmat))
    return outs[0] if len(outs) == 1 else jnp.concatenate(outs, axis=1)


def _rms_rows(x, g):
    return x * lax.rsqrt(jnp.mean(x * x, axis=-1, keepdims=True) + EPS) * g


def _rotate(x, rot_ref):
    c, s1, s2 = rot_ref[0], rot_ref[1], rot_ref[2]
    return x * c + pltpu.roll(x, 16, axis=1) * s1 + pltpu.roll(x, LANES - 16, axis=1) * s2


def _proj_a_kernel(x_ref, vec_ref, win_ref, wuq_ref, wuk_ref, wuv_ref, gmq_ref, gmc_ref, rot_ref,
                   tri_ref, pcum_ref,
                   ckv_o, kr_o, fk_o, fv_o, fl_o, qm_o, km_o, vm_o, qf_o, kf_o, vf_o,
                   carry_sc, *, tiles_per_seq):
    i = pl.program_id(0)
    x = x_ref[...]
    h = _rms_rows(x, vec_ref[0:1, :]).astype(BF16)
    z = _dot(h, win_ref[...])

    cqn = _rms_rows(z[:, 0:256], vec_ref[1:2, 0:256]).astype(BF16)
    c_kv = z[:, 256:512]
    ckv_o[...] = c_kv
    ckvn = _rms_rows(c_kv, vec_ref[1:2, 256:512]).astype(BF16)

    krb = z[:, 512:640]
    krb = krb * lax.rsqrt(jnp.sum(krb * krb, axis=-1, keepdims=True) * (1.0 / MLA_ROPE_DIM) + EPS)
    krb = _rotate(krb * vec_ref[1:2, 512:640], rot_ref)
    kr_o[...] = krb[:, 64:96]

    gmq = gmq_ref[...]
    q = _dot(cqn, wuq_ref[...])
    q = q * lax.rsqrt(_group_sumsq(q, gmq) * vec_ref[3:4, :] + EPS) * vec_ref[2:3, :]
    kx = _dot(ckvn, wuk_ref[...])
    kx = kx * lax.rsqrt(_group_sumsq(kx, gmq) * vec_ref[3:4, :] + EPS) * vec_ref[4:5, :]
    vx = _dot(ckvn, wuv_ref[...])
    for hh in range(MLA_HEADS):
        qb = q[:, LANES * hh:LANES * (hh + 1)]
        qm_o[hh] = _rotate(qb, rot_ref).astype(BF16)
        km_o[hh] = (kx[:, LANES * hh:LANES * (hh + 1)] + krb).astype(BF16)
        vm_o[hh] = vx[:, 64 * hh:64 * (hh + 1)].astype(BF16)

    gmc = gmc_ref[...]
    fq = z[:, 640:1152]
    fq = fq * lax.rsqrt(_group_sumsq(fq, gmc) * (1.0 / HEAD_DIM) + EPS) * vec_ref[5:6, 0:512]
    fk = z[:, 1152:1408]
    fk = fk * lax.rsqrt(_group_sumsq(fk, gmc) * (1.0 / HEAD_DIM) + EPS) * vec_ref[5:6, 512:768]
    fk_o[...] = fk
    fv = z[:, 1408:1664]
    fv_o[...] = fv
    f = z[:, 1664:1792] + vec_ref[1:2, 640:768]
    logf = jnp.minimum(f, 0.0) - jnp.log1p(jnp.exp(-jnp.abs(f)))
    fl_o[...] = logf[:, 0:FOX_HEADS]

    @pl.when(i % tiles_per_seq == 0)
    def _():
        carry_sc[...] = jnp.zeros_like(carry_sc)

    tri = tri_ref[...]
    lh, lm, ll = _split3(logf)
    cum = _dot(tri, lh) + _dot(tri, lm) + _dot(tri, ll) + carry_sc[...]
    carry_sc[...] = cum[ROW_TILE - 1:ROW_TILE, :]
    ch, cm, cl = _split3(cum)
    aug = _dot(ch, pcum_ref[0]) + _dot(cm, pcum_ref[1]) + _dot(cl, pcum_ref[2])

    for hh in range(FOX_HEADS):
        g = hh % 2
        qf_o[hh] = (_head_block(fq, hh, HEAD_DIM) + vec_ref[6 + g:7 + g, 0:LANES]).astype(BF16)
    for hk in range(FOX_KV_HEADS):
        kf_o[hk] = (_head_block(fk, hk, HEAD_DIM) + aug[:, LANES * hk:LANES * (hk + 1)]).astype(BF16)
        vf_o[hk] = fv[:, 64 * hk:64 * (hk + 1)].astype(BF16)


def _rot_tables(pos):
    half = MLA_ROPE_DIM // 2
    inv = jnp.exp(jnp.arange(half, dtype=F32) * (-2.0 * math.log(ROPE_THETA) / MLA_ROPE_DIM))
    ang = pos.astype(F32)[:, None] * inv[None, :]
    cos, sin = jnp.cos(ang), jnp.sin(ang)
    n = pos.shape[0]
    c = jnp.ones((n, LANES), F32).at[:, 64:80].set(cos).at[:, 80:96].set(cos)
    s1 = jnp.zeros((n, LANES), F32).at[:, 80:96].set(sin)
    s2 = jnp.zeros((n, LANES), F32).at[:, 64:80].set(-sin)
    return jnp.stack([c, s1, s2])


def _prep_a(i, norm_mix_l, w_in_a, mla_g_cq, mla_w_uq, mla_g_ckv, mla_w_ukv, mla_g_qn, mla_g_qr, mla_g_kn,
            mla_g_kr, fox_g_q, fox_g_k, fox_b_f):
    w = w_in_a[i]
    z128 = jnp.zeros((D_MODEL, LANES), F32)
    w_kr = z128.at[:, 64:96].set(w[:, 512:544])
    w_fl = z128.at[:, 0:FOX_HEADS].set(w[:, 1568:1576])
    win = jnp.concatenate([w[:, 0:512], w_kr, w[:, 544:1568], w_fl], axis=1).astype(BF16)

    wuq = mla_w_uq[i].reshape(MLA_Q_RANK, MLA_HEADS, 96)
    wuq = jnp.pad(wuq, ((0, 0), (0, 0), (0, 32))).reshape(MLA_Q_RANK, MLA_HEADS * LANES).astype(BF16)
    wukv = mla_w_ukv[i].reshape(MLA_KV_RANK, MLA_HEADS, 128)
    wuk = jnp.pad(wukv[:, :, 0:64], ((0, 0), (0, 0), (0, 64))).reshape(MLA_KV_RANK, MLA_HEADS * LANES).astype(BF16)
    wuv = wukv[:, :, 64:128].reshape(MLA_KV_RANK, MLA_HEADS * 64).astype(BF16)

    lane = np.arange(256) % LANES
    grp = np.where(lane < 64, 0, np.where(lane < 96, 1, -1)) + 2 * (np.arange(256) // LANES)
    valid = (lane < 96)
    gmq = ((grp[:, None] == grp[None, :]) & valid[:, None] & valid[None, :]).astype(np.float32)
    g64 = np.arange(256) // 64
    gmc = (g64[:, None] == g64[None, :]).astype(np.float32)

    zero32 = jnp.zeros((32,), F32)
    zero64 = jnp.zeros((64,), F32)
    qgain = jnp.tile(jnp.concatenate([mla_g_qn[i], mla_g_qr[i], zero32]) * MLA_SCALE, MLA_HEADS)
    invw = jnp.tile(jnp.concatenate([jnp.full((64,), 1 / 64., F32), jnp.full((32,), 1 / 32., F32), zero32]),
                    MLA_HEADS)
    kgain = jnp.tile(jnp.concatenate([mla_g_kn[i], zero64]), MLA_HEADS)
    g_kr_blk = jnp.zeros((LANES,), F32).at[64:96].set(mla_g_kr[i])
    b_f_blk = jnp.zeros((LANES,), F32).at[0:FOX_HEADS].set(fox_b_f[i])
    row1 = jnp.concatenate([mla_g_cq[i], mla_g_ckv[i], g_kr_blk, b_f_blk, jnp.zeros((256,), F32)])
    row5 = jnp.concatenate([jnp.tile(fox_g_q[i], FOX_HEADS) * ATTN_SCALE, jnp.tile(fox_g_k[i], FOX_KV_HEADS),
                            jnp.zeros((256,), F32)])
    qa0 = jnp.zeros((D_MODEL,), F32).at[64:67].set(1.0)
    qa1 = jnp.zeros((D_MODEL,), F32).at[67:70].set(1.0)
    vec = jnp.stack([norm_mix_l, row1, qgain, invw, kgain, row5, qa0, qa1])

    tri = np.tril(np.ones((ROW_TILE, ROW_TILE), np.float32))
    pcum = np.zeros((3, LANES, FOX_KV_HEADS * LANES), np.float32)
    for hk in range(FOX_KV_HEADS):
        for g in range(2):
            for j in range(3):
                pcum[j, 2 * hk + g, LANES * hk + 64 + 3 * g + j] = -1.0
    return dict(vec=vec, win=win, wuq=wuq, wuk=wuk, wuv=wuv, gmq=jnp.asarray(gmq, BF16),
                gmc=jnp.asarray(gmc, BF16), tri=jnp.asarray(tri, BF16), pcum=jnp.asarray(pcum, BF16))


def _const_spec(a):
    nd = a.ndim
    return pl.BlockSpec(a.shape, lambda *_: (0,) * nd)


def _proj_a(x2d, prm, rot, tiles_per_seq):
    n = x2d.shape[0]
    nt = n // ROW_TILE
    rt = rot.shape[1] // ROW_TILE
    row = lambda w: pl.BlockSpec((ROW_TILE, w), lambda i: (i, 0))
    hm = lambda nh, w: pl.BlockSpec((nh, ROW_TILE, w), lambda i: (0, i, 0))
    consts = [prm[k] for k in ("vec", "win", "wuq", "wuk", "wuv", "gmq", "gmc")]
    in_specs = ([row(D_MODEL)] + [_const_spec(a) for a in consts]
                + [pl.BlockSpec((3, ROW_TILE, LANES), lambda i: (0, i % rt, 0)),
                   _const_spec(prm["tri"]), _const_spec(prm["pcum"])])
    out_shape = [jax.ShapeDtypeStruct((n, 256), F32), jax.ShapeDtypeStruct((n, MLA_ROPE_DIM), F32),
                 jax.ShapeDtypeStruct((n, 256), F32), jax.ShapeDtypeStruct((n, 256), F32),
                 jax.ShapeDtypeStruct((n, FOX_HEADS), F32),
                 jax.ShapeDtypeStruct((MLA_HEADS, n, LANES), BF16), jax.ShapeDtypeStruct((MLA_HEADS, n, LANES), BF16),
                 jax.ShapeDtypeStruct((MLA_HEADS, n, 64), BF16),
                 jax.ShapeDtypeStruct((FOX_HEADS, n, LANES), BF16), jax.ShapeDtypeStruct((FOX_KV_HEADS, n, LANES), BF16),
                 jax.ShapeDtypeStruct((FOX_KV_HEADS, n, 64), BF16)]
    out_specs = [row(256), row(MLA_ROPE_DIM), row(256), row(256), row(FOX_HEADS),
                 hm(MLA_HEADS, LANES), hm(MLA_HEADS, LANES), hm(MLA_HEADS, 64),
                 hm(FOX_HEADS, LANES), hm(FOX_KV_HEADS, LANES), hm(FOX_KV_HEADS, 64)]
    return pl.pallas_call(
        functools.partial(_proj_a_kernel, tiles_per_seq=tiles_per_seq),
        grid=(nt,), in_specs=in_specs, out_specs=out_specs, out_shape=out_shape,
        scratch_shapes=[pltpu.VMEM((1, LANES), F32)],
        compiler_params=pltpu.CompilerParams(dimension_semantics=("arbitrary",), vmem_limit_bytes=VMEM_LIMIT),
        name="proj_a",
    )(x2d, *consts, rot, prm["tri"], prm["pcum"])


def _flash_kernel(q_ref, k_ref, v_ref, o_ref, m_sc, l_sc, acc_sc, *, tq, tk, kv_shared):
    qi = pl.program_id(2)
    ki = pl.program_id(3)
    last_k = (qi * tq + tq - 1) // tk

    @pl.when(ki == 0)
    def _():
        m_sc[...] = jnp.full_like(m_sc, M_INIT)
        l_sc[...] = jnp.zeros_like(l_sc)
        acc_sc[...] = jnp.zeros_like(acc_sc)

    def step(masked):
        for hh in range(2):
            kvh = 0 if kv_shared else hh
            s = _dot_nt(q_ref[hh], k_ref[kvh])
            if masked:
                rows = qi * tq + lax.broadcasted_iota(jnp.int32, s.shape, 0)
                cols = ki * tk + lax.broadcasted_iota(jnp.int32, s.shape, 1)
                s = jnp.where(cols <= rows, s, -jnp.inf)
            m_old = m_sc[hh]
            m_new = jnp.maximum(m_old, jnp.max(s, axis=-1, keepdims=True))
            alpha = jnp.exp(m_old - m_new)
            p = jnp.exp(s - m_new)
            l_sc[hh] = alpha * l_sc[hh] + jnp.sum(p, axis=-1, keepdims=True)
            acc_sc[hh] = alpha * acc_sc[hh] + _dot(p.astype(BF16), v_ref[kvh])
            m_sc[hh] = m_new

    needs_mask = ki * tk + tk - 1 > qi * tq

    @pl.when(jnp.logical_and(ki <= last_k, needs_mask))
    def _():
        step(True)

    @pl.when(jnp.logical_and(ki <= last_k, jnp.logical_not(needs_mask)))
    def _():
        step(False)

    @pl.when(ki == last_k)
    def _():
        for hh in range(2):
            o_ref[:, 64 * hh:64 * (hh + 1)] = (acc_sc[hh] / l_sc[hh]).astype(o_ref.dtype)


def _flash(q, k, v, batch, seq, *, tq, tk):
    hq, n, _ = q.shape
    hkv = k.shape[0]
    kv_shared = hkv * 2 == hq
    kvb = 1 if kv_shared else 2
    nq, nk = seq // tq, seq // tk

    def kv_map(b, hp, qi, ki):
        return (hp, b * nk + jnp.minimum(ki, (qi * tq + tq - 1) // tk), 0)

    return pl.pallas_call(
        functools.partial(_flash_kernel, tq=tq, tk=tk, kv_shared=kv_shared),
        grid=(batch, hq // 2, nq, nk),
        in_specs=[pl.BlockSpec((2, tq, LANES), lambda b, hp, qi, ki: (hp, b * nq + qi, 0)),
                  pl.BlockSpec((kvb, tk, LANES), kv_map),
                  pl.BlockSpec((kvb, tk, 64), kv_map)],
        out_specs=pl.BlockSpec((tq, LANES), lambda b, hp, qi, ki: (b * nq + qi, hp)),
        out_shape=jax.ShapeDtypeStruct((n, hq * 64), BF16),
        scratch_shapes=[pltpu.VMEM((2, tq, 1), F32), pltpu.VMEM((2, tq, 1), F32), pltpu.VMEM((2, tq, 64), F32)],
        compiler_params=pltpu.CompilerParams(
            dimension_semantics=("parallel", "parallel", "parallel", "arbitrary"), vmem_limit_bytes=VMEM_LIMIT),
        name="flash",
    )(q, k, v)


def _first_index_of_max(vals, lane):
    vmax = jnp.max(vals, axis=-1, keepdims=True)
    idx = jnp.min(jnp.where(vals == vmax, lane, 4 * LANES), axis=-1, keepdims=True)
    return vmax, idx


def _route(hm, wr_ref, br_ref):
    hh, hl = hm.astype(BF16), None
    hl = (hm - hh.astype(F32)).astype(BF16)
    logits = _dot(hh, wr_ref[0]) + _dot(hh, wr_ref[1]) + _dot(hl, wr_ref[0]) + br_ref[...]
    lane = _lane_iota(logits.shape)
    gl = jnp.where(lane < MOE_GROUPS, logits, -jnp.inf)
    gmax, grp = _first_index_of_max(gl, lane)
    p_grp = 1.0 / jnp.sum(jnp.exp(gl - gmax), axis=-1, keepdims=True)
    lo = MOE_GROUPS + MOE_PER_GROUP * grp
    sl = jnp.where(jnp.logical_and(lane >= lo, lane < lo + MOE_PER_GROUP), logits, -jnp.inf)
    v1, i1 = _first_index_of_max(sl, lane)
    sl2 = jnp.where(lane == i1, -jnp.inf, sl)
    v2, i2 = _first_index_of_max(sl2, lane)
    e2 = jnp.exp(v2 - v1)
    w1 = p_grp / (1.0 + e2)
    w2 = p_grp * e2 / (1.0 + e2)
    return jnp.where(lane == i1, w1, 0.0) + jnp.where(lane == i2, w2, 0.0)


def _out_moe_kernel(x_ref, mix_ref, wout_ref, gffn_ref, wr_ref, br_ref, w1_ref, w3_ref, w2_ref, o_ref,
                    xn_sc, hm_sc, gate_sc, acc_sc):
    e = pl.program_id(1)

    @pl.when(e == 0)
    def _():
        xn = x_ref[...] + _dot(mix_ref[...], wout_ref[...])
        xn_sc[...] = xn
        hm = _rms_rows(xn, gffn_ref[...])
        hm_sc[...] = hm.astype(BF16)
        gate_sc[...] = _route(hm, wr_ref, br_ref)
        acc_sc[...] = jnp.zeros_like(acc_sc)

    hb = hm_sc[...]
    gate = gate_sc[...]
    ge = jnp.sum(jnp.where(_lane_iota(gate.shape) == MOE_GROUPS + e, gate, 0.0), axis=-1, keepdims=True)
    h1 = _dot(hb, w1_ref[0])
    h3 = _dot(hb, w3_ref[0])
    a = (h1 * jax.nn.sigmoid(h1)) * h3 * ge
    acc_sc[...] += _dot(a.astype(BF16), w2_ref[0])

    @pl.when(e == MOE_EXPERTS - 1)
    def _():
        o_ref[...] = xn_sc[...] + acc_sc[...]


def _prep_moe(l, norm_ffn, moe_w_group, moe_b_group, moe_w_sub, moe_b_sub, moe_w1, moe_w3, moe_w2):
    wr = jnp.zeros((D_MODEL, LANES), F32)
    wr = wr.at[:, 0:MOE_GROUPS].set(moe_w_group[l])
    wr = wr.at[:, MOE_GROUPS:MOE_GROUPS + MOE_EXPERTS].set(
        jnp.transpose(moe_w_sub[l], (1, 0, 2)).reshape(D_MODEL, MOE_EXPERTS))
    wr_hi = wr.astype(BF16)
    wr_lo = (wr - wr_hi.astype(F32)).astype(BF16)
    br = jnp.zeros((1, LANES), F32)
    br = br.at[0, 0:MOE_GROUPS].set(moe_b_group[l])
    br = br.at[0, MOE_GROUPS:MOE_GROUPS + MOE_EXPERTS].set(moe_b_sub[l].reshape(-1))
    return dict(gffn=norm_ffn[l][None, :], wr=jnp.stack([wr_hi, wr_lo]), br=br,
                w1=moe_w1[l].astype(BF16), w3=moe_w3[l].astype(BF16), w2=moe_w2[l].astype(BF16))


def _out_moe(x2d, mix, w_out, prm, tm):
    n = x2d.shape[0]
    row = lambda w: pl.BlockSpec((tm, w), lambda i, e: (i, 0))
    cst = lambda a: pl.BlockSpec(a.shape, lambda i, e: (0,) * a.ndim)
    return pl.pallas_call(
        _out_moe_kernel,
        grid=(n // tm, MOE_EXPERTS),
        in_specs=[row(D_MODEL), row(mix.shape[1]), cst(w_out), cst(prm["gffn"]), cst(prm["wr"]), cst(prm["br"]),
                  pl.BlockSpec((1, D_MODEL, MOE_FF), lambda i, e: (e, 0, 0)),
                  pl.BlockSpec((1, D_MODEL, MOE_FF), lambda i, e: (e, 0, 0)),
                  pl.BlockSpec((1, MOE_FF, D_MODEL), lambda i, e: (e, 0, 0))],
        out_specs=row(D_MODEL),
        out_shape=jax.ShapeDtypeStruct((n, D_MODEL), F32),
        scratch_shapes=[pltpu.VMEM((tm, D_MODEL), F32), pltpu.VMEM((tm, D_MODEL), BF16),
                        pltpu.VMEM((tm, LANES), F32), pltpu.VMEM((tm, D_MODEL), F32)],
        compiler_params=pltpu.CompilerParams(dimension_semantics=("parallel", "arbitrary"),
                                             vmem_limit_bytes=VMEM_LIMIT),
        name="out_moe",
    )(x2d, mix, w_out, prm["gffn"], prm["wr"], prm["br"], prm["w1"], prm["w3"], prm["w2"])


def _proj_b_kernel(x_ref, vec_ref, win_ref, gmc_ref, wmix_ref, bmix_ref,
                   q_o, k_o, v_o, gv_o, gm_o, kmean_o, ka_o, va_o, *, tiles_per_seq):
    i = pl.program_id(0)
    x = x_ref[...]
    h = _rms_rows(x, vec_ref[0:1, :]).astype(BF16)
    z = _dot(h, win_ref[...])
    gmc = gmc_ref[...]
    q = z[:, 0:512]
    q = q * lax.rsqrt(_group_sumsq(q, gmc) * (1.0 / HEAD_DIM) + EPS) * vec_ref[1:2, 0:512]
    q_o[...] = q
    k = z[:, 512:768]
    k = k * lax.rsqrt(_group_sumsq(k, gmc) * (1.0 / HEAD_DIM) + EPS) * vec_ref[1:2, 512:768]
    k_o[...] = k
    v = z[:, 768:1024]
    v_o[...] = v
    kmean_o[0] = jnp.mean(k, axis=0, keepdims=True)
    blk = i % tiles_per_seq
    onehot = jnp.where(_lane_iota((ROW_TILE, LANES)) == 64 + blk, 1.0, 0.0)
    for hk in range(MOBA_KV_HEADS):
        ka_o[hk] = (_head_block(k, hk, HEAD_DIM) + onehot).astype(BF16)
        va_o[hk] = v[:, 64 * hk:64 * (hk + 1)].astype(BF16)

    zz = z[:, 1024:2048]
    zz = zz * (0.5 * (1.0 + jnp.tanh(math.sqrt(2.0 / math.pi) * (zz + 0.044715 * (zz * zz * zz)))))
    u = zz[:, 0:GM_WIDTH]
    g = zz[:, GM_WIDTH:]
    gc = g - jnp.mean(g, axis=-1, keepdims=True)
    gv = gc * lax.rsqrt(jnp.mean(gc * gc, axis=-1, keepdims=True) + EPS) * vec_ref[2:3, 0:512] + vec_ref[2:3, 512:1024]
    gv_o[...] = gv
    gvb = gv.astype(BF16)
    for gg in range(GM_GROUPS):
        sl = slice(GM_GROUP_DIM * gg, GM_GROUP_DIM * (gg + 1))
        mixg = _dot(wmix_ref[gg], gvb[:, sl]) + bmix_ref[gg]
        gm_o[:, sl] = (u[:, sl] * mixg).astype(BF16)


def _prep_b(i, norm_mix_l, w_in_b, moba_g_q, moba_g_k, gm_ln_g, gm_ln_b):
    row1 = jnp.concatenate([jnp.tile(moba_g_q[i], MOBA_HEADS), jnp.tile(moba_g_k[i], MOBA_KV_HEADS),
                            jnp.zeros((256,), F32)])
    row2 = jnp.concatenate([gm_ln_g[i], gm_ln_b[i]])
    g64 = np.arange(256) // 64
    gmc = (g64[:, None] == g64[None, :]).astype(np.float32)
    return dict(vec=jnp.stack([norm_mix_l, row1, row2]), win=w_in_b[i].astype(BF16), gmc=jnp.asarray(gmc, BF16))


def _gmlp_mix_weights(w_s, b_s, pos, period):
    n = pos.shape[0]
    cp, ch, seq = pos % GM_CHUNK, pos // GM_CHUNK, np.arange(n) // period
    mask = (ch[:, None] == ch[None, :]) & (pos[None, :] <= pos[:, None]) & (seq[:, None] == seq[None, :])
    c0 = cp[:period]
    assert all((cp[i * period:(i + 1) * period] == c0).all() for i in range(n // period))
    if (np.diff(c0) == 1).all():
        w0 = w_s[:, c0[0]:c0[0] + period, c0[0]:c0[0] + period]
        b0 = b_s[:, c0[0]:c0[0] + period]
    else:
        w0 = w_s[:, c0[:, None], c0[None, :]]
        b0 = b_s[:, c0]
    w = jnp.where(mask[None], jnp.tile(w0, (1, n // period, n // period)), 0.0)
    b = jnp.broadcast_to(jnp.tile(b0, (1, n // period))[:, :, None], (GM_GROUPS, n, GM_GROUP_DIM))
    return w, b


def _proj_b(x2d, prm, wmix, bmix, tiles_per_seq):
    n = x2d.shape[0]
    nt = n // ROW_TILE
    row = lambda w: pl.BlockSpec((ROW_TILE, w), lambda i: (i, 0))
    hm = lambda nh, w: pl.BlockSpec((nh, ROW_TILE, w), lambda i: (0, i, 0))
    consts = [prm["vec"], prm["win"], prm["gmc"], wmix, bmix]
    sds = jax.ShapeDtypeStruct
    return pl.pallas_call(
        functools.partial(_proj_b_kernel, tiles_per_seq=tiles_per_seq),
        grid=(nt,),
        in_specs=[row(D_MODEL)] + [_const_spec(a) for a in consts],
        out_specs=[row(512), row(256), row(256), row(GM_WIDTH), row(GM_WIDTH),
                   pl.BlockSpec((1, 1, 256), lambda i: (i, 0, 0)), hm(MOBA_KV_HEADS, LANES), hm(MOBA_KV_HEADS, 64)],
        out_shape=[sds((n, 512), F32), sds((n, 256), F32), sds((n, 256), F32), sds((n, GM_WIDTH), F32),
                   sds((n, GM_WIDTH), BF16), sds((nt, 1, 256), F32),
                   sds((MOBA_KV_HEADS, n, LANES), BF16), sds((MOBA_KV_HEADS, n, 64), BF16)],
        compiler_params=pltpu.CompilerParams(dimension_semantics=("parallel",), vmem_limit_bytes=VMEM_LIMIT),
        name="proj_b",
    )(x2d, *consts)


def _moba_gate_kernel(q_ref, km_ref, qa_o, *, tiles_per_seq):
    cur = pl.program_id(0) % tiles_per_seq
    q = q_ref[...]
    qh = q.astype(BF16)
    ql = (q - qh.astype(F32)).astype(BF16)
    gate = _dot(qh, km_ref[0, 0]) + _dot(qh, km_ref[0, 1]) + _dot(ql, km_ref[0, 0])
    lane = _lane_iota(gate.shape)
    n = lane % MOBA_MAX_BLOCKS
    valid = n < cur
    g = jnp.where(valid, gate, -jnp.inf)
    rank = jnp.zeros(gate.shape, jnp.int32)
    for j in range(1, MOBA_MAX_BLOCKS):
        wrapped = n + j >= MOBA_MAX_BLOCKS
        pv = jnp.where(wrapped, pltpu.roll(g, MOBA_MAX_BLOCKS - j, axis=1), pltpu.roll(g, LANES - j, axis=1))
        beats = jnp.logical_or(pv > g, jnp.logical_and(pv == g, wrapped))
        rank = rank + beats.astype(jnp.int32)
    sel = jnp.logical_or(jnp.logical_and(valid, rank < MOBA_TOPK), n == cur)
    sb = jnp.where(sel, 0.0, NEG_BIG)
    qs = q * ATTN_SCALE
    for hh in range(MOBA_HEADS):
        shift = (64 - MOBA_MAX_BLOCKS * hh) % LANES
        sbh = pltpu.roll(sb, shift, axis=1) if shift else sb
        sbh = jnp.where(jnp.logical_and(lane >= 64, lane < 64 + MOBA_MAX_BLOCKS), sbh, 0.0)
        qa_o[hh] = (_head_block(qs, hh, HEAD_DIM) + sbh).astype(BF16)


def _moba_km(kmean, batch, nb):
    km = kmean.reshape(batch, nb, MOBA_KV_HEADS, HEAD_DIM)
    km = jnp.repeat(km, MOBA_HEADS // MOBA_KV_HEADS, axis=2)
    km = jnp.pad(jnp.transpose(km, (0, 2, 3, 1)), ((0, 0), (0, 0), (0, 0), (0, MOBA_MAX_BLOCKS - nb)))
    km = jnp.einsum("bhdn,hg->bhdgn", km, jnp.eye(MOBA_HEADS, dtype=F32)).reshape(batch, 512, LANES)
    hi = km.astype(BF16)
    lo = (km - hi.astype(F32)).astype(BF16)
    return jnp.stack([hi, lo], axis=1)


def _moba_gate(q, km, tiles_per_seq):
    n = q.shape[0]
    return pl.pallas_call(
        functools.partial(_moba_gate_kernel, tiles_per_seq=tiles_per_seq),
        grid=(n // ROW_TILE,),
        in_specs=[pl.BlockSpec((ROW_TILE, 512), lambda i: (i, 0)),
                  pl.BlockSpec((1, 2, 512, LANES), lambda i: (i // tiles_per_seq, 0, 0, 0))],
        out_specs=pl.BlockSpec((MOBA_HEADS, ROW_TILE, LANES), lambda i: (0, i, 0)),
        out_shape=jax.ShapeDtypeStruct((MOBA_HEADS, n, LANES), BF16),
        compiler_params=pltpu.CompilerParams(dimension_semantics=("parallel",), vmem_limit_bytes=VMEM_LIMIT),
        name="moba_gate",
    )(q, km)


PAGE_CHUNK = 16


class _PagedCaches:
    def __init__(self, pt_ref, caches, bufs, sems, layer, pps):
        self.pt, self.caches, self.bufs, self.sems, self.layer, self.pps = pt_ref, caches, bufs, sems, layer, pps

    def _copy(self, a, page, slot, p):
        return pltpu.make_async_copy(self.caches[a].at[self.layer, page], self.bufs[a].at[slot, p],
                                     self.sems.at[a, slot])

    def start(self, s, j, slot):
        for p in range(self.pps):
            page = self.pt[s, j * self.pps + p]
            for a in range(len(self.caches)):
                self._copy(a, page, slot, p).start()

    def wait(self, slot):
        for a in range(len(self.caches)):
            for p in range(self.pps):
                self._copy(a, 0, slot, p).wait()

    def step(self):
        s, j = pl.program_id(0), pl.program_id(1)
        ns, nj = pl.num_programs(0), pl.num_programs(1)
        t = s * nj + j
        slot = t % 2

        @pl.when(t == 0)
        def _():
            self.start(s, j, slot)

        @pl.when(t + 1 < ns * nj)
        def _():
            wrap = j + 1 == nj
            self.start(jnp.where(wrap, s + 1, s), jnp.where(wrap, 0, j + 1), 1 - slot)

        self.wait(slot)
        return slot

    def pages(self, a, slot, lo, hi, axis):
        parts = [self.bufs[a][slot, p] for p in range(lo, hi)]
        return parts[0] if len(parts) == 1 else jnp.concatenate(parts, axis=axis)


def _paged_scratch(shapes, pps):
    return ([pltpu.VMEM((2, pps) + shp, F32) for shp in shapes]
            + [pltpu.SemaphoreType.DMA((len(shapes), 2))])


def _softmax_step(s, pv, m, l, acc):
    m_new = jnp.maximum(m, jnp.max(s, axis=-1, keepdims=True))
    alpha = jnp.exp(m - m_new)
    p = jnp.exp(s - m_new)
    return m_new, alpha * l + jnp.sum(p, axis=-1, keepdims=True), alpha * acc + pv(p.astype(BF16))


def _new_row_mask(shape, dec_seq):
    qidx = lax.broadcasted_iota(jnp.int32, shape, 0) % dec_seq
    return lax.broadcasted_iota(jnp.int32, shape, 1) <= qidx


def _mla_dec_kernel(pt_ref, ckv_hbm, krt_hbm, q_ref, cnew_ref, krnew_ref, wukg_ref, wukt_ref, wuv_ref, g_ref, o_ref,
                    qp_sc, qr_sc, m_sc, l_sc, acc_sc, ckv_buf, krt_buf, sems, *, layer, pps, dec_seq):
    j = pl.program_id(1)
    rows = MLA_HEADS * dec_seq
    paged = _PagedCaches(pt_ref, (ckv_hbm, krt_hbm), (ckv_buf, krt_buf), sems, layer, pps)
    slot = paged.step()

    @pl.when(j == 0)
    def _():
        for hh in range(MLA_HEADS):
            qh = q_ref[hh]
            qp_sc[dec_seq * hh:dec_seq * (hh + 1), :] = _dot(qh[:, 0:64].astype(BF16), wukg_ref[hh])
            qr_sc[dec_seq * hh:dec_seq * (hh + 1), :] = qh[:, 64:96]
        m_sc[...] = jnp.full_like(m_sc, M_INIT)
        l_sc[...] = jnp.zeros_like(l_sc)
        acc_sc[...] = jnp.zeros_like(acc_sc)

    lhs = jnp.concatenate([qp_sc[...].astype(BF16), wukt_ref[...]], axis=0)
    qr = qr_sc[...].astype(BF16)
    g = g_ref[...]

    def chunk(c, krt, carry, masked):
        keys = c.shape[0]
        cn = (c * lax.rsqrt(jnp.mean(c * c, axis=-1, keepdims=True) + EPS) * g).astype(BF16)
        big = _dot_nt(lhs, cn)
        kexp = big[rows:, :]
        ss = jnp.sum((kexp * kexp).reshape(MLA_HEADS, MLA_NOPE_DIM, keys), axis=1)
        r = lax.rsqrt(ss * (1.0 / MLA_NOPE_DIM) + EPS)
        s = big[0:rows, :].reshape(MLA_HEADS, dec_seq, keys) * r[:, None, :]
        s = s.reshape(rows, keys) + _dot(qr, krt.astype(BF16))
        if masked:
            s = jnp.where(_new_row_mask(s.shape, dec_seq), s, -jnp.inf)
        return _softmax_step(s, lambda p: _dot(p, cn), *carry)

    carry = (m_sc[...], l_sc[...], acc_sc[...])
    for p0 in range(0, pps, PAGE_CHUNK):
        p1 = min(p0 + PAGE_CHUNK, pps)
        carry = chunk(paged.pages(0, slot, p0, p1, 0), paged.pages(1, slot, p0, p1, 1), carry, False)
    m_sc[...], l_sc[...], acc_sc[...] = carry

    @pl.when(j == pl.num_programs(1) - 1)
    def _():
        m, l, acc = chunk(cnew_ref[0], krnew_ref[0], (m_sc[...], l_sc[...], acc_sc[...]), True)
        lat = acc / l
        for hh in range(MLA_HEADS):
            lat_h = lat[dec_seq * hh:dec_seq * (hh + 1), :].astype(BF16)
            o_ref[0, :, 64 * hh:64 * (hh + 1)] = _dot(lat_h, wuv_ref[hh])


def _mla_decode(cache_ckv, cache_kr, layer, page_table, qm, cnew, krnew, wukg, wukt, wuv, g_ckv, pps, dec_seq):
    n_seq, n_pages = page_table.shape
    rows = MLA_HEADS * dec_seq
    per_seq = lambda shp: pl.BlockSpec((1,) + shp, lambda s, j, pt: (s,) + (0,) * len(shp))
    cst = lambda a: pl.BlockSpec(a.shape, lambda s, j, pt: (0,) * a.ndim)
    hbm = pl.BlockSpec(memory_space=pl.ANY)
    grid_spec = pltpu.PrefetchScalarGridSpec(
        num_scalar_prefetch=1, grid=(n_seq, n_pages // pps),
        in_specs=[hbm, hbm, pl.BlockSpec((MLA_HEADS, dec_seq, LANES), lambda s, j, pt: (0, s, 0)),
                  per_seq((LANES, MLA_KV_RANK)), per_seq((MLA_ROPE_DIM, LANES)),
                  cst(wukg), cst(wukt), cst(wuv), cst(g_ckv)],
        out_specs=per_seq((dec_seq, MLA_HEADS * MLA_V_DIM)),
        scratch_shapes=[pltpu.VMEM((rows, MLA_KV_RANK), F32), pltpu.VMEM((rows, MLA_ROPE_DIM), F32),
                        pltpu.VMEM((rows, 1), F32), pltpu.VMEM((rows, 1), F32), pltpu.VMEM((rows, MLA_KV_RANK), F32)]
        + _paged_scratch([(LANES, MLA_KV_RANK), (MLA_ROPE_DIM, LANES)], pps))
    return pl.pallas_call(
        functools.partial(_mla_dec_kernel, layer=layer, pps=pps, dec_seq=dec_seq),
        grid_spec=grid_spec,
        out_shape=jax.ShapeDtypeStruct((n_seq, dec_seq, MLA_HEADS * MLA_V_DIM), F32),
        compiler_params=pltpu.CompilerParams(dimension_semantics=("arbitrary", "arbitrary"),
                                             vmem_limit_bytes=VMEM_LIMIT),
        name="mla_decode",
    )(page_table, cache_ckv, cache_kr, qm, cnew, krnew, wukg, wukt, wuv, g_ckv)


def _fox_dec_kernel(pt_ref, k_hbm, v_hbm, lf_hbm, qbd_ref, knew_ref, vnew_ref, lfnew_ref, utri_ref, o_ref,
                    m_sc, l_sc, acc_sc, base_sc, k_buf, v_buf, lf_buf, sems, *, layer, pps, dec_seq):
    j = pl.program_id(1)
    rows = FOX_HEADS * dec_seq
    paged = _PagedCaches(pt_ref, (k_hbm, v_hbm, lf_hbm), (k_buf, v_buf, lf_buf), sems, layer, pps)
    slot = paged.step()

    @pl.when(j == 0)
    def _():
        m_sc[...] = jnp.full_like(m_sc, M_INIT)
        l_sc[...] = jnp.zeros_like(l_sc)
        acc_sc[...] = jnp.zeros_like(acc_sc)
        base_sc[...] = jnp.zeros_like(base_sc)

    qbd = qbd_ref[0]
    utri = utri_ref[...]

    def cum_lanes(lf):
        a, b, c = _split3(lf)
        return _dot(a, utri) + _dot(b, utri) + _dot(c, utri)

    def chunk(kt, vt, c, carry, masked):
        keys = kt.shape[1]
        s = _dot(qbd, kt.astype(BF16)).reshape(FOX_HEADS, dec_seq, keys) - c[:, None, :]
        s = s.reshape(rows, keys)
        if masked:
            s = jnp.where(_new_row_mask(s.shape, dec_seq), s, -jnp.inf)
        vtb = vt.astype(BF16)
        return _softmax_step(s, lambda p: _dot_nt(p, vtb), *carry)

    cum = cum_lanes(paged.pages(2, slot, 0, pps, 0))
    base = base_sc[...]
    c_pages = []
    for p in range(pps):
        c_loc = cum[FOX_HEADS * p:FOX_HEADS * (p + 1), :]
        c_pages.append(c_loc + base)
        base = base + c_loc[:, LANES - 1:LANES]
    base_sc[...] = base
    carry = (m_sc[...], l_sc[...], acc_sc[...])
    for p0 in range(0, pps, PAGE_CHUNK):
        p1 = min(p0 + PAGE_CHUNK, pps)
        c = c_pages[p0] if p1 - p0 == 1 else jnp.concatenate(c_pages[p0:p1], axis=1)
        carry = chunk(paged.pages(0, slot, p0, p1, 1), paged.pages(1, slot, p0, p1, 1), c, carry, False)
    m_sc[...], l_sc[...], acc_sc[...] = carry

    @pl.when(j == pl.num_programs(1) - 1)
    def _():
        c_new = cum_lanes(lfnew_ref[0]) + base_sc[...]
        m, l, acc = chunk(knew_ref[0], vnew_ref[0], c_new, (m_sc[...], l_sc[...], acc_sc[...]), True)
        o = acc / l
        for hq in range(FOX_HEADS):
            hk = hq // 2
            o_ref[0, :, 64 * hq:64 * (hq + 1)] = o[dec_seq * hq:dec_seq * (hq + 1), 64 * hk:64 * (hk + 1)]


def _fox_decode(cache_k, cache_v, cache_lft, layer, page_table, qbd, knew, vnew, lfnew, pps, dec_seq):
    n_seq, n_pages = page_table.shape
    rows = FOX_HEADS * dec_seq
    per_seq = lambda shp: pl.BlockSpec((1,) + shp, lambda s, j, pt: (s,) + (0,) * len(shp))
    utri = jnp.asarray(np.triu(np.ones((LANES, LANES), np.float32)), BF16)
    hbm = pl.BlockSpec(memory_space=pl.ANY)
    grid_spec = pltpu.PrefetchScalarGridSpec(
        num_scalar_prefetch=1, grid=(n_seq, n_pages // pps),
        in_specs=[hbm, hbm, hbm,
                  per_seq((rows, 256)), per_seq((256, LANES)), per_seq((256, LANES)), per_seq((FOX_HEADS, LANES)),
                  pl.BlockSpec((LANES, LANES), lambda s, j, pt: (0, 0))],
        out_specs=per_seq((dec_seq, FOX_HEADS * HEAD_DIM)),
        scratch_shapes=[pltpu.VMEM((rows, 1), F32), pltpu.VMEM((rows, 1), F32), pltpu.VMEM((rows, 256), F32),
                        pltpu.VMEM((FOX_HEADS, 1), F32)]
        + _paged_scratch([(256, LANES), (256, LANES), (FOX_HEADS, LANES)], pps))
    return pl.pallas_call(
        functools.partial(_fox_dec_kernel, layer=layer, pps=pps, dec_seq=dec_seq),
        grid_spec=grid_spec,
        out_shape=jax.ShapeDtypeStruct((n_seq, dec_seq, FOX_HEADS * HEAD_DIM), F32),
        compiler_params=pltpu.CompilerParams(dimension_semantics=("arbitrary", "arbitrary"),
                                             vmem_limit_bytes=VMEM_LIMIT),
        name="fox_decode",
    )(page_table, cache_k, cache_v, cache_lft, qbd, knew, vnew, lfnew, utri)


def _moba_dec_kernel(pt_ref, k_hbm, v_hbm, qbd_ref, qg_ref, knew_ref, vnew_ref, o_ref,
                     m_sc, l_sc, kmt_sc, o_sc, k_buf, v_buf, sems, *, layer, pps, dec_seq, nb):
    j = pl.program_id(1)
    rows = MOBA_HEADS * dec_seq
    bps = pps // 2
    paged = _PagedCaches(pt_ref, (k_hbm, v_hbm), (k_buf, v_buf), sems, layer, pps)
    slot = paged.step()
    qbd = qbd_ref[0]
    lane = _lane_iota((rows, LANES))
    lane_k = _lane_iota((256, LANES))

    @pl.when(j == 0)
    def _():
        m_sc[...] = jnp.zeros_like(m_sc)
        l_sc[...] = jnp.zeros_like(l_sc)
        kmt_sc[...] = jnp.zeros_like(kmt_sc)

    m_all, l_all, kmt = m_sc[...], l_sc[...], kmt_sc[...]
    for b in range(bps):
        n = j * bps + b
        kt = paged.pages(0, slot, 2 * b, 2 * b + 2, 1)
        vt = paged.pages(1, slot, 2 * b, 2 * b + 2, 1)
        s = _dot(qbd, kt.astype(BF16))
        mn = jnp.max(s, axis=-1, keepdims=True)
        p = jnp.exp(s - mn)
        o_sc[n] = _dot_nt(p.astype(BF16), vt.astype(BF16))
        here = lane == n
        m_all = jnp.where(here, mn, m_all)
        l_all = jnp.where(here, jnp.sum(p, axis=-1, keepdims=True), l_all)
        kmt = jnp.where(lane_k == n, jnp.sum(kt, axis=-1, keepdims=True) * (1.0 / MOBA_BLOCK), kmt)
    m_sc[...], l_sc[...], kmt_sc[...] = m_all, l_all, kmt

    @pl.when(j == pl.num_programs(1) - 1)
    def _():
        s = _dot(qbd, knew_ref[0].astype(BF16))
        s = jnp.where(_new_row_mask(s.shape, dec_seq), s, -jnp.inf)
        m_o = jnp.max(s, axis=-1, keepdims=True)
        p = jnp.exp(s - m_o)
        l_o = jnp.sum(p, axis=-1, keepdims=True)
        o_o = _dot_nt(p.astype(BF16), vnew_ref[0].astype(BF16))
        qg = qg_ref[0]
        qh = qg.astype(BF16)
        ql = (qg - qh.astype(F32)).astype(BF16)
        kh = kmt.astype(BF16)
        kl = (kmt - kh.astype(F32)).astype(BF16)
        g_all = _dot(qh, kh) + _dot(qh, kl) + _dot(ql, kh)
        g = jnp.where(lane < nb, g_all, -jnp.inf)
        sel = jnp.zeros(g.shape, jnp.bool_)
        for _ in range(min(MOBA_TOPK, nb)):
            vmax, idx = _first_index_of_max(g, lane)
            hit = lane == idx
            sel = jnp.logical_or(sel, jnp.logical_and(hit, vmax > -jnp.inf))
            g = jnp.where(hit, -jnp.inf, g)
        m_top = jnp.maximum(jnp.max(jnp.where(sel, m_all, -jnp.inf), axis=-1, keepdims=True), m_o)
        w = jnp.where(sel, jnp.exp(m_all - m_top), 0.0)
        w_o = jnp.exp(m_o - m_top)
        den = jnp.sum(w * l_all, axis=-1, keepdims=True) + w_o * l_o

        def body(nn, acc):
            col = jnp.sum(jnp.where(lane == nn, w, 0.0), axis=-1, keepdims=True)
            return acc + col * o_sc[nn]

        o = lax.fori_loop(0, nb, body, w_o * o_o, unroll=8) / den
        for hq in range(MOBA_HEADS):
            hk = hq // 2
            o_ref[0, :, 64 * hq:64 * (hq + 1)] = o[dec_seq * hq:dec_seq * (hq + 1), 64 * hk:64 * (hk + 1)]


def _moba_decode(cache_k, cache_v, layer, page_table, qbd, qg, knew, vnew, pps, dec_seq):
    n_seq, n_pages = page_table.shape
    rows = MOBA_HEADS * dec_seq
    nb = n_pages // 2
    assert nb <= LANES
    per_seq = lambda shp: pl.BlockSpec((1,) + shp, lambda s, j, pt: (s,) + (0,) * len(shp))
    hbm = pl.BlockSpec(memory_space=pl.ANY)
    grid_spec = pltpu.PrefetchScalarGridSpec(
        num_scalar_prefetch=1, grid=(n_seq, n_pages // pps),
        in_specs=[hbm, hbm, per_seq((rows, 256)), per_seq((rows, 256)), per_seq((256, LANES)), per_seq((256, LANES))],
        out_specs=per_seq((dec_seq, MOBA_HEADS * HEAD_DIM)),
        scratch_shapes=[pltpu.VMEM((rows, LANES), F32), pltpu.VMEM((rows, LANES), F32), pltpu.VMEM((256, LANES), F32),
                        pltpu.VMEM((nb, rows, 256), F32)]
        + _paged_scratch([(256, LANES), (256, LANES)], pps))
    return pl.pallas_call(
        functools.partial(_moba_dec_kernel, layer=layer, pps=pps, dec_seq=dec_seq, nb=nb),
        grid_spec=grid_spec,
        out_shape=jax.ShapeDtypeStruct((n_seq, dec_seq, MOBA_HEADS * HEAD_DIM), F32),
        compiler_params=pltpu.CompilerParams(dimension_semantics=("arbitrary", "arbitrary"),
                                             vmem_limit_bytes=VMEM_LIMIT),
        name="moba_decode",
    )(page_table, cache_k, cache_v, qbd, qg, knew, vnew)


def _block_diag_queries(q4):
    nh = q4.shape[0]
    place = jnp.asarray(np.arange(nh)[:, None] // 2 == np.arange(nh // 2)[None, :], q4.dtype)
    out = jnp.einsum("hsqd,hk->shqkd", q4, place)
    return out.reshape(q4.shape[1], nh * q4.shape[2], (nh // 2) * 64)


def _pad_rows(a, rows):
    return jnp.pad(a, ((0, 0), (0, rows - a.shape[1]), (0, 0)))


def kernel(x_prompt, x_sample, cache_mla_ckv, cache_mla_krope, cache_fox_k, cache_fox_v, cache_fox_logf, cache_moba_k, cache_moba_v, page_table, norm_mix, norm_ffn, w_in_a, mla_g_cq, mla_w_uq, mla_g_ckv, mla_w_ukv, mla_g_qn, mla_g_qr, mla_g_kn, mla_g_kr, fox_g_q, fox_g_k, fox_b_f, w_out_a, w_in_b, moba_g_q, moba_g_k, gm_ln_g, gm_ln_b, gm_w_s, gm_b_s, w_out_b, moe_w_group, moe_b_group, moe_w_sub, moe_b_sub, moe_w1, moe_w3, moe_w2):
    B, T, _ = x_prompt.shape
    S, DS, _ = x_sample.shape
    n_pages = page_table.shape[1]
    page = cache_mla_ckv.shape[2]
    past = n_pages * page
    n_phys = cache_mla_ckv.shape[1]
    assert page == LANES and T % 1024 == 0 and T // MOBA_BLOCK <= MOBA_MAX_BLOCKS
    assert (S * DS) % ROW_TILE == 0 and ROW_TILE % DS == 0 and DS <= LANES
    assert past % MOBA_BLOCK == 0 and past % GM_CHUNK == 0 and n_pages % 2 == 0
    pps = math.gcd(16, n_pages)
    tps = T // ROW_TILE
    np_rows, ns_rows = B * T, S * DS
    xp = x_prompt.reshape(np_rows, D_MODEL)
    xs = x_sample.reshape(ns_rows, D_MODEL)
    pos_s = past + np.arange(ROW_TILE) % DS

    pa = _prep_a(0, norm_mix[0], w_in_a, mla_g_cq, mla_w_uq, mla_g_ckv, mla_w_ukv, mla_g_qn, mla_g_qr, mla_g_kn,
                 mla_g_kr, fox_g_q, fox_g_k, fox_b_f)
    p_ckv, p_kr, p_fk, p_fv, p_fl, qm, km, vm, qf, kf, vf = _proj_a(xp, pa, _rot_tables(jnp.arange(T)), tps)
    s_ckv, s_kr, s_fk, s_fv, s_fl, qm_s, _, _, qf_s, _, _ = _proj_a(xs, pa, _rot_tables(jnp.asarray(pos_s)), 1)
    mix_p = jnp.concatenate([_flash(qm, km, vm, B, T, tq=1024, tk=1024),
                             _flash(qf, kf, vf, B, T, tq=1024, tk=1024)], axis=1)

    wukv = mla_w_ukv[0].reshape(MLA_KV_RANK, MLA_HEADS, 128)
    wuk_t = jnp.transpose(wukv[:, :, 0:64], (1, 2, 0))
    wukg = (wuk_t * mla_g_kn[0][None, :, None]).astype(BF16)
    wukt = wuk_t.reshape(MLA_HEADS * 64, MLA_KV_RANK).astype(BF16)
    wuv = jnp.transpose(wukv[:, :, 64:128], (1, 0, 2)).astype(BF16)
    rows_last = lambda c: jnp.moveaxis(c, 2, -1).reshape(c.shape[0], n_phys, -1, page)
    new_t = lambda a: jnp.pad(jnp.transpose(a.reshape(S, DS, -1), (0, 2, 1)), ((0, 0), (0, 0), (0, LANES - DS)))
    o_mla = _mla_decode(cache_mla_ckv, rows_last(cache_mla_krope), 0, page_table, qm_s.astype(F32),
                        _pad_rows(s_ckv.reshape(S, DS, -1), LANES), new_t(s_kr),
                        wukg, wukt, wuv, mla_g_ckv[0][None, :], pps, DS)
    qbd_f = _block_diag_queries(qf_s[:, :, 0:64].reshape(FOX_HEADS, S, DS, 64))
    o_fox = _fox_decode(rows_last(cache_fox_k), rows_last(cache_fox_v), rows_last(cache_fox_logf),
                        0, page_table, qbd_f, new_t(s_fk), new_t(s_fv), new_t(s_fl), pps, DS)
    mix_s = jnp.concatenate([o_mla.reshape(ns_rows, -1), o_fox.reshape(ns_rows, -1)], axis=1).astype(BF16)

    pm = _prep_moe(0, norm_ffn, moe_w_group, moe_b_group, moe_w_sub, moe_b_sub, moe_w1, moe_w3, moe_w2)
    w_out = w_out_a[0].astype(BF16)
    xp = _out_moe(xp, mix_p, w_out, pm, min(1024, np_rows))
    xs = _out_moe(xs, mix_s, w_out, pm, min(1024, ns_rows))

    pb = _prep_b(0, norm_mix[1], w_in_b, moba_g_q, moba_g_k, gm_ln_g, gm_ln_b)
    wmix_p, bmix_p = _gmlp_mix_weights(gm_w_s[0], gm_b_s[0], np.arange(ROW_TILE), GM_CHUNK)
    wmix_s, bmix_s = _gmlp_mix_weights(gm_w_s[0], gm_b_s[0], pos_s, DS)
    q_p, p_mk, p_mv, _, gm_p, kmean, ka, va = _proj_b(xp, pb, wmix_p.astype(BF16), bmix_p, tps)
    q_s, s_mk, s_mv, s_gv, gm_s, _, _, _ = _proj_b(xs, pb, wmix_s.astype(BF16), bmix_s, 1)
    qa = _moba_gate(q_p, _moba_km(kmean, B, tps), tps)
    mix_p = jnp.concatenate([_flash(qa, ka, va, B, T, tq=1024, tk=1024), gm_p], axis=1)

    q4 = jnp.transpose(q_s.reshape(S, DS, MOBA_HEADS, 64), (2, 0, 1, 3))
    o_moba = _moba_decode(rows_last(cache_moba_k), rows_last(cache_moba_v), 0,
                          page_table, _block_diag_queries((q4 * ATTN_SCALE).astype(BF16)), _block_diag_queries(q4),
                          new_t(s_mk), new_t(s_mv), pps, DS)
    mix_s = jnp.concatenate([o_moba.reshape(ns_rows, -1).astype(BF16), gm_s], axis=1)

    pm = _prep_moe(1, norm_ffn, moe_w_group, moe_b_group, moe_w_sub, moe_b_sub, moe_w1, moe_w3, moe_w2)
    w_out = w_out_b[0].astype(BF16)
    xp = _out_moe(xp, mix_p, w_out, pm, min(1024, np_rows))
    xs = _out_moe(xs, mix_s, w_out, pm, min(1024, ns_rows))

    pr = lambda a, *tail: a.reshape((1, B, T) + tail)
    sr = lambda a, *tail: a.reshape((1, S, DS) + tail)
    return (xp.reshape(B, T, D_MODEL), xs.reshape(S, DS, D_MODEL),
            pr(p_ckv, MLA_KV_RANK), pr(p_kr, MLA_ROPE_DIM), pr(p_fk, FOX_KV_HEADS, 64), pr(p_fv, FOX_KV_HEADS, 64),
            pr(p_fl, FOX_HEADS), pr(p_mk, MOBA_KV_HEADS, 64), pr(p_mv, MOBA_KV_HEADS, 64),
            sr(s_ckv, MLA_KV_RANK), sr(s_kr, MLA_ROPE_DIM), sr(s_fk, FOX_KV_HEADS, 64), sr(s_fv, FOX_KV_HEADS, 64),
            sr(s_fl, FOX_HEADS), sr(s_mk, MOBA_KV_HEADS, 64), sr(s_mv, MOBA_KV_HEADS, 64), sr(s_gv, GM_WIDTH))
```

```python
import functools
import math

import jax
import jax.numpy as jnp
import numpy as np
from jax import lax
from jax.experimental import pallas as pl
from jax.experimental.pallas import tpu as pltpu

F32 = jnp.float32
BF16 = jnp.bfloat16

D_MODEL = 1024
HEAD_DIM = 64
MLA_HEADS = 8
MLA_Q_RANK = 256
MLA_KV_RANK = 256
MLA_NOPE_DIM = 64
MLA_ROPE_DIM = 32
MLA_V_DIM = 64
ROPE_THETA = 10000.0
FOX_HEADS = 8
FOX_KV_HEADS = 4
MOBA_HEADS = 8
MOBA_KV_HEADS = 4
MOBA_BLOCK = 256
MOBA_TOPK = 3
GM_GROUPS = 4
GM_GROUP_DIM = 128
GM_WIDTH = GM_GROUPS * GM_GROUP_DIM
GM_CHUNK = 128
MOE_GROUPS = 4
MOE_PER_GROUP = 4
MOE_EXPERTS = 16
MOE_FF = 256
EPS = 1e-6
MLA_SCALE = (MLA_NOPE_DIM + MLA_ROPE_DIM) ** -0.5
ATTN_SCALE = HEAD_DIM ** -0.5

LANES = 128
ROW_TILE = 256
FLASH_TILE = 1024
MOBA_MAX_BLOCKS = 16
NEG_BIG = -(2.0 ** 100)
M_INIT = -1e30
VMEM_LIMIT = 48 * 1024 * 1024


def _dot(a, b):
    return jnp.dot(a, b, preferred_element_type=F32)


def _dot_nt(a, b):
    return lax.dot_general(a, b, (((1,), (1,)), ((), ())), preferred_element_type=F32)


def _split3(x):
    hi = x.astype(BF16)
    r = x - hi.astype(F32)
    mid = r.astype(BF16)
    lo = (r - mid.astype(F32)).astype(BF16)
    return hi, mid, lo


def _lane_iota(shape):
    return lax.broadcasted_iota(jnp.int32, shape, len(shape) - 1)


def _head_block(x, h, width):
    start = h * width
    blk = x[:, (start // LANES) * LANES:(start // LANES + 1) * LANES]
    off = start % LANES
    if off:
        blk = pltpu.roll(blk, LANES - off, axis=1)
    return jnp.where(_lane_iota(blk.shape) < width, blk, 0.0)


def _value_block(v, h):
    blk = _head_block(v, h, HEAD_DIM)
    return jnp.where(_lane_iota(blk.shape) == HEAD_DIM, 1.0, blk).astype(BF16)


def _group_sumsq(x, gmat):
    outs = []
    for j in range(x.shape[1] // 256):
        xs = x[:, 256 * j:256 * (j + 1)]
        outs.append(_dot((xs * xs).astype(BF16), gmat))
    return outs[0] if len(outs) == 1 else jnp.concatenate(outs, axis=1)


def _rms_rows(x, g):
    return x * lax.rsqrt(jnp.mean(x * x, axis=-1, keepdims=True) + EPS) * g


def _rotate(x, rot_ref):
    c, s1, s2 = rot_ref[0], rot_ref[1], rot_ref[2]
    return x * c + pltpu.roll(x, 16, axis=1) * s1 + pltpu.roll(x, LANES - 16, axis=1) * s2


def _proj_a_kernel(x_ref, vec_ref, win_ref, wuq_ref, wuk_ref, wuv_ref, gmq_ref, gmc_ref, rot_ref,
                   tri_ref, pcum_ref,
                   ckv_o, kr_o, fk_o, fv_o, fl_o, qm_o, km_o, vm_o, qf_o, kf_o, vf_o,
                   carry_sc, *, tiles_per_seq):
    i = pl.program_id(0)
    x = x_ref[...]
    h = _rms_rows(x, vec_ref[0:1, :]).astype(BF16)
    z = _dot(h, win_ref[...])

    cqn = _rms_rows(z[:, 0:256], vec_ref[1:2, 0:256]).astype(BF16)
    c_kv = z[:, 256:512]
    ckv_o[...] = c_kv
    ckvn = _rms_rows(c_kv, vec_ref[1:2, 256:512]).astype(BF16)

    krb = z[:, 512:640]
    krb = krb * lax.rsqrt(jnp.sum(krb * krb, axis=-1, keepdims=True) * (1.0 / MLA_ROPE_DIM) + EPS)
    krb = _rotate(krb * vec_ref[1:2, 512:640], rot_ref)
    kr_o[...] = krb[:, 64:96]

    gmq = gmq_ref[...]
    q = _dot(cqn, wuq_ref[...])
    q = q * lax.rsqrt(_group_sumsq(q, gmq) * vec_ref[3:4, :] + EPS) * vec_ref[2:3, :]
    kx = _dot(ckvn, wuk_ref[...])
    kx = kx * lax.rsqrt(_group_sumsq(kx, gmq) * vec_ref[3:4, :] + EPS) * vec_ref[4:5, :]
    vx = _dot(ckvn, wuv_ref[...])
    for hh in range(MLA_HEADS):
        qb = q[:, LANES * hh:LANES * (hh + 1)]
        qm_o[hh] = _rotate(qb, rot_ref).astype(BF16)
        km_o[hh] = (kx[:, LANES * hh:LANES * (hh + 1)] + krb).astype(BF16)
        vm_o[hh] = _value_block(vx, hh)

    gmc = gmc_ref[...]
    fq = z[:, 640:1152]
    fq = fq * lax.rsqrt(_group_sumsq(fq, gmc) * (1.0 / HEAD_DIM) + EPS) * vec_ref[5:6, 0:512]
    fk = z[:, 1152:1408]
    fk = fk * lax.rsqrt(_group_sumsq(fk, gmc) * (1.0 / HEAD_DIM) + EPS) * vec_ref[5:6, 512:768]
    fk_o[...] = fk
    fv = z[:, 1408:1664]
    fv_o[...] = fv
    f = z[:, 1664:1792] + vec_ref[1:2, 640:768]
    logf = jnp.minimum(f, 0.0) - jnp.log1p(jnp.exp(-jnp.abs(f)))
    fl_o[...] = logf[:, 0:FOX_HEADS]

    @pl.when(i % tiles_per_seq == 0)
    def _():
        carry_sc[...] = jnp.zeros_like(carry_sc)

    tri = tri_ref[...]
    lh, lm, ll = _split3(logf)
    cum = _dot(tri, lh) + _dot(tri, lm) + _dot(tri, ll) + carry_sc[...]
    carry_sc[...] = cum[ROW_TILE - 1:ROW_TILE, :]
    ch, cm, cl = _split3(cum)
    aug = _dot(ch, pcum_ref[0]) + _dot(cm, pcum_ref[1]) + _dot(cl, pcum_ref[2])

    for hh in range(FOX_HEADS):
        g = hh % 2
        qf_o[hh] = (_head_block(fq, hh, HEAD_DIM) + vec_ref[6 + g:7 + g, 0:LANES]).astype(BF16)
    for hk in range(FOX_KV_HEADS):
        kf_o[hk] = (_head_block(fk, hk, HEAD_DIM) + aug[:, LANES * hk:LANES * (hk + 1)]).astype(BF16)
        vf_o[hk] = _value_block(fv, hk)


def _rot_tables(pos):
    half = MLA_ROPE_DIM // 2
    inv = jnp.exp(jnp.arange(half, dtype=F32) * (-2.0 * math.log(ROPE_THETA) / MLA_ROPE_DIM))
    ang = pos.astype(F32)[:, None] * inv[None, :]
    cos, sin = jnp.cos(ang), jnp.sin(ang)
    n = pos.shape[0]
    c = jnp.ones((n, LANES), F32).at[:, 64:80].set(cos).at[:, 80:96].set(cos)
    s1 = jnp.zeros((n, LANES), F32).at[:, 80:96].set(sin)
    s2 = jnp.zeros((n, LANES), F32).at[:, 64:80].set(-sin)
    return jnp.stack([c, s1, s2])


def _prep_a(i, norm_mix_l, w_in_a, mla_g_cq, mla_w_uq, mla_g_ckv, mla_w_ukv, mla_g_qn, mla_g_qr, mla_g_kn,
            mla_g_kr, fox_g_q, fox_g_k, fox_b_f):
    w = w_in_a[i]
    z128 = jnp.zeros((D_MODEL, LANES), F32)
    w_kr = z128.at[:, 64:96].set(w[:, 512:544])
    w_fl = z128.at[:, 0:FOX_HEADS].set(w[:, 1568:1576])
    win = jnp.concatenate([w[:, 0:512], w_kr, w[:, 544:1568], w_fl], axis=1).astype(BF16)

    wuq = mla_w_uq[i].reshape(MLA_Q_RANK, MLA_HEADS, 96)
    wuq = jnp.pad(wuq, ((0, 0), (0, 0), (0, 32))).reshape(MLA_Q_RANK, MLA_HEADS * LANES).astype(BF16)
    wukv = mla_w_ukv[i].reshape(MLA_KV_RANK, MLA_HEADS, 128)
    wuk = jnp.pad(wukv[:, :, 0:64], ((0, 0), (0, 0), (0, 64))).reshape(MLA_KV_RANK, MLA_HEADS * LANES).astype(BF16)
    wuv = wukv[:, :, 64:128].reshape(MLA_KV_RANK, MLA_HEADS * 64).astype(BF16)

    lane = np.arange(256) % LANES
    grp = np.where(lane < 64, 0, np.where(lane < 96, 1, -1)) + 2 * (np.arange(256) // LANES)
    valid = (lane < 96)
    gmq = ((grp[:, None] == grp[None, :]) & valid[:, None] & valid[None, :]).astype(np.float32)
    g64 = np.arange(256) // 64
    gmc = (g64[:, None] == g64[None, :]).astype(np.float32)

    zero32 = jnp.zeros((32,), F32)
    zero64 = jnp.zeros((64,), F32)
    qgain = jnp.tile(jnp.concatenate([mla_g_qn[i], mla_g_qr[i], zero32]) * MLA_SCALE, MLA_HEADS)
    invw = jnp.tile(jnp.concatenate([jnp.full((64,), 1 / 64., F32), jnp.full((32,), 1 / 32., F32), zero32]),
                    MLA_HEADS)
    kgain = jnp.tile(jnp.concatenate([mla_g_kn[i], zero64]), MLA_HEADS)
    g_kr_blk = jnp.zeros((LANES,), F32).at[64:96].set(mla_g_kr[i])
    b_f_blk = jnp.zeros((LANES,), F32).at[0:FOX_HEADS].set(fox_b_f[i])
    row1 = jnp.concatenate([mla_g_cq[i], mla_g_ckv[i], g_kr_blk, b_f_blk, jnp.zeros((256,), F32)])
    row5 = jnp.concatenate([jnp.tile(fox_g_q[i], FOX_HEADS) * ATTN_SCALE, jnp.tile(fox_g_k[i], FOX_KV_HEADS),
                            jnp.zeros((256,), F32)])
    qa0 = jnp.zeros((D_MODEL,), F32).at[64:67].set(1.0)
    qa1 = jnp.zeros((D_MODEL,), F32).at[67:70].set(1.0)
    vec = jnp.stack([norm_mix_l, row1, qgain, invw, kgain, row5, qa0, qa1])

    tri = np.tril(np.ones((ROW_TILE, ROW_TILE), np.float32))
    pcum = np.zeros((3, LANES, FOX_KV_HEADS * LANES), np.float32)
    for hk in range(FOX_KV_HEADS):
        for g in range(2):
            for j in range(3):
                pcum[j, 2 * hk + g, LANES * hk + 64 + 3 * g + j] = -1.0
    return dict(vec=vec, win=win, wuq=wuq, wuk=wuk, wuv=wuv, gmq=jnp.asarray(gmq, BF16),
                gmc=jnp.asarray(gmc, BF16), tri=jnp.asarray(tri, BF16), pcum=jnp.asarray(pcum, BF16))


def _const_spec(a):
    nd = a.ndim
    return pl.BlockSpec(a.shape, lambda *_: (0,) * nd)


def _proj_a(x2d, prm, rot, tiles_per_seq):
    n = x2d.shape[0]
    nt = n // ROW_TILE
    rt = rot.shape[1] // ROW_TILE
    row = lambda w: pl.BlockSpec((ROW_TILE, w), lambda i: (i, 0))
    hm = lambda nh, w: pl.BlockSpec((nh, ROW_TILE, w), lambda i: (0, i, 0))
    consts = [prm[k] for k in ("vec", "win", "wuq", "wuk", "wuv", "gmq", "gmc")]
    in_specs = ([row(D_MODEL)] + [_const_spec(a) for a in consts]
                + [pl.BlockSpec((3, ROW_TILE, LANES), lambda i: (0, i % rt, 0)),
                   _const_spec(prm["tri"]), _const_spec(prm["pcum"])])
    out_shape = [jax.ShapeDtypeStruct((n, 256), F32), jax.ShapeDtypeStruct((n, MLA_ROPE_DIM), F32),
                 jax.ShapeDtypeStruct((n, 256), F32), jax.ShapeDtypeStruct((n, 256), F32),
                 jax.ShapeDtypeStruct((n, FOX_HEADS), F32),
                 jax.ShapeDtypeStruct((MLA_HEADS, n, LANES), BF16), jax.ShapeDtypeStruct((MLA_HEADS, n, LANES), BF16),
                 jax.ShapeDtypeStruct((MLA_HEADS, n, LANES), BF16),
                 jax.ShapeDtypeStruct((FOX_HEADS, n, LANES), BF16), jax.ShapeDtypeStruct((FOX_KV_HEADS, n, LANES), BF16),
                 jax.ShapeDtypeStruct((FOX_KV_HEADS, n, LANES), BF16)]
    out_specs = [row(256), row(MLA_ROPE_DIM), row(256), row(256), row(FOX_HEADS),
                 hm(MLA_HEADS, LANES), hm(MLA_HEADS, LANES), hm(MLA_HEADS, LANES),
                 hm(FOX_HEADS, LANES), hm(FOX_KV_HEADS, LANES), hm(FOX_KV_HEADS, LANES)]
    return pl.pallas_call(
        functools.partial(_proj_a_kernel, tiles_per_seq=tiles_per_seq),
        grid=(nt,), in_specs=in_specs, out_specs=out_specs, out_shape=out_shape,
        scratch_shapes=[pltpu.VMEM((1, LANES), F32)],
        compiler_params=pltpu.CompilerParams(dimension_semantics=("arbitrary",), vmem_limit_bytes=VMEM_LIMIT),
        name="proj_a",
    )(x2d, *consts, rot, prm["tri"], prm["pcum"])


def _flash_kernel(q_ref, k_ref, v_ref, o_ref, m_sc, acc_sc, *, tile, kv_shared):
    qi = pl.program_id(2)
    ki = pl.program_id(3)

    @pl.when(ki == 0)
    def _():
        m_sc[...] = jnp.full_like(m_sc, M_INIT)
        acc_sc[...] = jnp.zeros_like(acc_sc)

    def update(hh, r0, nr, nk, masked):
        kvh = 0 if kv_shared else hh
        s = _dot_nt(q_ref[hh, r0:r0 + nr, :], k_ref[kvh, 0:nk, :])
        if masked:
            rows = r0 + lax.broadcasted_iota(jnp.int32, s.shape, 0)
            s = jnp.where(lax.broadcasted_iota(jnp.int32, s.shape, 1) <= rows, s, -jnp.inf)
        m_old = m_sc[hh, r0:r0 + nr, :]
        m_new = jnp.maximum(m_old, jnp.max(s, axis=-1, keepdims=True))
        p = jnp.exp(s - m_new).astype(BF16)
        acc_sc[hh, r0:r0 + nr, :] = jnp.exp(m_old - m_new) * acc_sc[hh, r0:r0 + nr, :] + _dot(p, v_ref[kvh, 0:nk, :])
        m_sc[hh, r0:r0 + nr, :] = m_new

    @pl.when(ki == qi)
    def _():
        half = tile // 2
        for hh in range(2):
            update(hh, 0, half, half, True)
            update(hh, half, half, tile, True)

    @pl.when(ki < qi)
    def _():
        for hh in range(2):
            update(hh, 0, tile, tile, False)

    @pl.when(ki == qi)
    def _():
        for hh in range(2):
            acc = acc_sc[hh]
            o_ref[:, 64 * hh:64 * (hh + 1)] = (acc[:, 0:HEAD_DIM] / acc[:, HEAD_DIM:HEAD_DIM + 1]).astype(o_ref.dtype)


def _flash(q, k, v, batch, seq, *, tile):
    hq, n, _ = q.shape
    hkv = k.shape[0]
    kv_shared = hkv * 2 == hq
    kvb = 1 if kv_shared else 2
    nt = seq // tile

    def kv_map(b, hp, qi, ki):
        return (hp, b * nt + jnp.minimum(ki, qi), 0)

    return pl.pallas_call(
        functools.partial(_flash_kernel, tile=tile, kv_shared=kv_shared),
        grid=(batch, hq // 2, nt, nt),
        in_specs=[pl.BlockSpec((2, tile, LANES), lambda b, hp, qi, ki: (hp, b * nt + qi, 0)),
                  pl.BlockSpec((kvb, tile, LANES), kv_map),
                  pl.BlockSpec((kvb, tile, LANES), kv_map)],
        out_specs=pl.BlockSpec((tile, LANES), lambda b, hp, qi, ki: (b * nt + qi, hp)),
        out_shape=jax.ShapeDtypeStruct((n, hq * 64), BF16),
        scratch_shapes=[pltpu.VMEM((2, tile, 1), F32), pltpu.VMEM((2, tile, LANES), F32)],
        compiler_params=pltpu.CompilerParams(
            dimension_semantics=("parallel", "parallel", "parallel", "arbitrary"), vmem_limit_bytes=VMEM_LIMIT),
        name="flash",
    )(q, k, v)


def _first_index_of_max(vals, lane):
    vmax = jnp.max(vals, axis=-1, keepdims=True)
    idx = jnp.min(jnp.where(vals == vmax, lane, 4 * LANES), axis=-1, keepdims=True)
    return vmax, idx


def _route(hm, wr_ref, br_ref):
    hh, hl = hm.astype(BF16), None
    hl = (hm - hh.astype(F32)).astype(BF16)
    logits = _dot(hh, wr_ref[0]) + _dot(hh, wr_ref[1]) + _dot(hl, wr_ref[0]) + br_ref[...]
    lane = _lane_iota(logits.shape)
    gl = jnp.where(lane < MOE_GROUPS, logits, -jnp.inf)
    gmax, grp = _first_index_of_max(gl, lane)
    p_grp = 1.0 / jnp.sum(jnp.exp(gl - gmax), axis=-1, keepdims=True)
    lo = MOE_GROUPS + MOE_PER_GROUP * grp
    sl = jnp.where(jnp.logical_and(lane >= lo, lane < lo + MOE_PER_GROUP), logits, -jnp.inf)
    v1, i1 = _first_index_of_max(sl, lane)
    sl2 = jnp.where(lane == i1, -jnp.inf, sl)
    v2, i2 = _first_index_of_max(sl2, lane)
    e2 = jnp.exp(v2 - v1)
    w1 = p_grp / (1.0 + e2)
    w2 = p_grp * e2 / (1.0 + e2)
    return jnp.where(lane == i1, w1, 0.0) + jnp.where(lane == i2, w2, 0.0)


def _out_moe_kernel(x_ref, mix_ref, wout_ref, gffn_ref, wr_ref, br_ref, w1_ref, w3_ref, w2_ref, o_ref,
                    xn_sc, hm_sc, gate_sc, acc_sc):
    e = pl.program_id(1)

    @pl.when(e == 0)
    def _():
        xn = x_ref[...] + _dot(mix_ref[...], wout_ref[...])
        xn_sc[...] = xn
        hm = _rms_rows(xn, gffn_ref[...])
        hm_sc[...] = hm.astype(BF16)
        gate_sc[...] = _route(hm, wr_ref, br_ref)
        acc_sc[...] = jnp.zeros_like(acc_sc)

    hb = hm_sc[...]
    gate = gate_sc[...]
    ge = jnp.sum(jnp.where(_lane_iota(gate.shape) == MOE_GROUPS + e, gate, 0.0), axis=-1, keepdims=True)
    h1 = _dot(hb, w1_ref[0])
    h3 = _dot(hb, w3_ref[0])
    a = (h1 * jax.nn.sigmoid(h1)) * h3 * ge
    acc_sc[...] += _dot(a.astype(BF16), w2_ref[0])

    @pl.when(e == MOE_EXPERTS - 1)
    def _():
        o_ref[...] = xn_sc[...] + acc_sc[...]


def _prep_moe(l, norm_ffn, moe_w_group, moe_b_group, moe_w_sub, moe_b_sub, moe_w1, moe_w3, moe_w2):
    wr = jnp.zeros((D_MODEL, LANES), F32)
    wr = wr.at[:, 0:MOE_GROUPS].set(moe_w_group[l])
    wr = wr.at[:, MOE_GROUPS:MOE_GROUPS + MOE_EXPERTS].set(
        jnp.transpose(moe_w_sub[l], (1, 0, 2)).reshape(D_MODEL, MOE_EXPERTS))
    wr_hi = wr.astype(BF16)
    wr_lo = (wr - wr_hi.astype(F32)).astype(BF16)
    br = jnp.zeros((1, LANES), F32)
    br = br.at[0, 0:MOE_GROUPS].set(moe_b_group[l])
    br = br.at[0, MOE_GROUPS:MOE_GROUPS + MOE_EXPERTS].set(moe_b_sub[l].reshape(-1))
    return dict(gffn=norm_ffn[l][None, :], wr=jnp.stack([wr_hi, wr_lo]), br=br,
                w1=moe_w1[l].astype(BF16), w3=moe_w3[l].astype(BF16), w2=moe_w2[l].astype(BF16))


def _out_moe(x2d, mix, w_out, prm, tm):
    n = x2d.shape[0]
    row = lambda w: pl.BlockSpec((tm, w), lambda i, e: (i, 0))
    cst = lambda a: pl.BlockSpec(a.shape, lambda i, e: (0,) * a.ndim)
    return pl.pallas_call(
        _out_moe_kernel,
        grid=(n // tm, MOE_EXPERTS),
        in_specs=[row(D_MODEL), row(mix.shape[1]), cst(w_out), cst(prm["gffn"]), cst(prm["wr"]), cst(prm["br"]),
                  pl.BlockSpec((1, D_MODEL, MOE_FF), lambda i, e: (e, 0, 0)),
                  pl.BlockSpec((1, D_MODEL, MOE_FF), lambda i, e: (e, 0, 0)),
                  pl.BlockSpec((1, MOE_FF, D_MODEL), lambda i, e: (e, 0, 0))],
        out_specs=row(D_MODEL),
        out_shape=jax.ShapeDtypeStruct((n, D_MODEL), F32),
        scratch_shapes=[pltpu.VMEM((tm, D_MODEL), F32), pltpu.VMEM((tm, D_MODEL), BF16),
                        pltpu.VMEM((tm, LANES), F32), pltpu.VMEM((tm, D_MODEL), F32)],
        compiler_params=pltpu.CompilerParams(dimension_semantics=("parallel", "arbitrary"),
                                             vmem_limit_bytes=VMEM_LIMIT),
        name="out_moe",
    )(x2d, mix, w_out, prm["gffn"], prm["wr"], prm["br"], prm["w1"], prm["w3"], prm["w2"])


def _proj_b_kernel(x_ref, vec_ref, win_ref, gmc_ref, wmix_ref, bmix_ref,
                   q_o, k_o, v_o, gv_o, gm_o, kmean_o, ka_o, va_o, *, tiles_per_seq):
    i = pl.program_id(0)
    x = x_ref[...]
    h = _rms_rows(x, vec_ref[0:1, :]).astype(BF16)
    z = _dot(h, win_ref[...])
    gmc = gmc_ref[...]
    q = z[:, 0:512]
    q = q * lax.rsqrt(_group_sumsq(q, gmc) * (1.0 / HEAD_DIM) + EPS) * vec_ref[1:2, 0:512]
    q_o[...] = q
    k = z[:, 512:768]
    k = k * lax.rsqrt(_group_sumsq(k, gmc) * (1.0 / HEAD_DIM) + EPS) * vec_ref[1:2, 512:768]
    k_o[...] = k
    v = z[:, 768:1024]
    v_o[...] = v
    kmean_o[0] = jnp.mean(k, axis=0, keepdims=True)
    blk = i % tiles_per_seq
    onehot = jnp.where(_lane_iota((ROW_TILE, LANES)) == 64 + blk, 1.0, 0.0)
    for hk in range(MOBA_KV_HEADS):
        ka_o[hk] = (_head_block(k, hk, HEAD_DIM) + onehot).astype(BF16)
        va_o[hk] = _value_block(v, hk)

    zz = z[:, 1024:2048]
    zz = zz * (0.5 * (1.0 + jnp.tanh(math.sqrt(2.0 / math.pi) * (zz + 0.044715 * (zz * zz * zz)))))
    u = zz[:, 0:GM_WIDTH]
    g = zz[:, GM_WIDTH:]
    gc = g - jnp.mean(g, axis=-1, keepdims=True)
    gv = gc * lax.rsqrt(jnp.mean(gc * gc, axis=-1, keepdims=True) + EPS) * vec_ref[2:3, 0:512] + vec_ref[2:3, 512:1024]
    gv_o[...] = gv
    gvb = gv.astype(BF16)
    for gg in range(GM_GROUPS):
        sl = slice(GM_GROUP_DIM * gg, GM_GROUP_DIM * (gg + 1))
        mixg = _dot(wmix_ref[gg], gvb[:, sl]) + bmix_ref[gg]
        gm_o[:, sl] = (u[:, sl] * mixg).astype(BF16)


def _prep_b(i, norm_mix_l, w_in_b, moba_g_q, moba_g_k, gm_ln_g, gm_ln_b):
    row1 = jnp.concatenate([jnp.tile(moba_g_q[i], MOBA_HEADS), jnp.tile(moba_g_k[i], MOBA_KV_HEADS),
                            jnp.zeros((256,), F32)])
    row2 = jnp.concatenate([gm_ln_g[i], gm_ln_b[i]])
    g64 = np.arange(256) // 64
    gmc = (g64[:, None] == g64[None, :]).astype(np.float32)
    return dict(vec=jnp.stack([norm_mix_l, row1, row2]), win=w_in_b[i].astype(BF16), gmc=jnp.asarray(gmc, BF16))


def _gmlp_mix_weights(w_s, b_s, pos, period):
    n = pos.shape[0]
    cp, ch, seq = pos % GM_CHUNK, pos // GM_CHUNK, np.arange(n) // period
    mask = (ch[:, None] == ch[None, :]) & (pos[None, :] <= pos[:, None]) & (seq[:, None] == seq[None, :])
    c0 = cp[:period]
    assert all((cp[i * period:(i + 1) * period] == c0).all() for i in range(n // period))
    if (np.diff(c0) == 1).all():
        w0 = w_s[:, c0[0]:c0[0] + period, c0[0]:c0[0] + period]
        b0 = b_s[:, c0[0]:c0[0] + period]
    else:
        w0 = w_s[:, c0[:, None], c0[None, :]]
        b0 = b_s[:, c0]
    w = jnp.where(mask[None], jnp.tile(w0, (1, n // period, n // period)), 0.0)
    b = jnp.broadcast_to(jnp.tile(b0, (1, n // period))[:, :, None], (GM_GROUPS, n, GM_GROUP_DIM))
    return w, b


def _proj_b(x2d, prm, wmix, bmix, tiles_per_seq):
    n = x2d.shape[0]
    nt = n // ROW_TILE
    row = lambda w: pl.BlockSpec((ROW_TILE, w), lambda i: (i, 0))
    hm = lambda nh, w: pl.BlockSpec((nh, ROW_TILE, w), lambda i: (0, i, 0))
    consts = [prm["vec"], prm["win"], prm["gmc"], wmix, bmix]
    sds = jax.ShapeDtypeStruct
    return pl.pallas_call(
        functools.partial(_proj_b_kernel, tiles_per_seq=tiles_per_seq),
        grid=(nt,),
        in_specs=[row(D_MODEL)] + [_const_spec(a) for a in consts],
        out_specs=[row(512), row(256), row(256), row(GM_WIDTH), row(GM_WIDTH),
                   pl.BlockSpec((1, 1, 256), lambda i: (i, 0, 0)), hm(MOBA_KV_HEADS, LANES), hm(MOBA_KV_HEADS, LANES)],
        out_shape=[sds((n, 512), F32), sds((n, 256), F32), sds((n, 256), F32), sds((n, GM_WIDTH), F32),
                   sds((n, GM_WIDTH), BF16), sds((nt, 1, 256), F32),
                   sds((MOBA_KV_HEADS, n, LANES), BF16), sds((MOBA_KV_HEADS, n, LANES), BF16)],
        compiler_params=pltpu.CompilerParams(dimension_semantics=("parallel",), vmem_limit_bytes=VMEM_LIMIT),
        name="proj_b",
    )(x2d, *consts)


def _moba_gate_kernel(q_ref, km_ref, qa_o, *, tiles_per_seq):
    cur = pl.program_id(0) % tiles_per_seq
    q = q_ref[...]
    qh = q.astype(BF16)
    ql = (q - qh.astype(F32)).astype(BF16)
    gate = _dot(qh, km_ref[0, 0]) + _dot(qh, km_ref[0, 1]) + _dot(ql, km_ref[0, 0])
    lane = _lane_iota(gate.shape)
    n = lane % MOBA_MAX_BLOCKS
    valid = n < cur
    g = jnp.where(valid, gate, -jnp.inf)
    rank = jnp.zeros(gate.shape, jnp.int32)
    for j in range(1, MOBA_MAX_BLOCKS):
        wrapped = n + j >= MOBA_MAX_BLOCKS
        pv = jnp.where(wrapped, pltpu.roll(g, MOBA_MAX_BLOCKS - j, axis=1), pltpu.roll(g, LANES - j, axis=1))
        beats = jnp.logical_or(pv > g, jnp.logical_and(pv == g, wrapped))
        rank = rank + beats.astype(jnp.int32)
    sel = jnp.logical_or(jnp.logical_and(valid, rank < MOBA_TOPK), n == cur)
    sb = jnp.where(sel, 0.0, NEG_BIG)
    qs = q * ATTN_SCALE
    for hh in range(MOBA_HEADS):
        shift = (64 - MOBA_MAX_BLOCKS * hh) % LANES
        sbh = pltpu.roll(sb, shift, axis=1) if shift else sb
        sbh = jnp.where(jnp.logical_and(lane >= 64, lane < 64 + MOBA_MAX_BLOCKS), sbh, 0.0)
        qa_o[hh] = (_head_block(qs, hh, HEAD_DIM) + sbh).astype(BF16)


def _moba_km(kmean, batch, nb):
    km = kmean.reshape(batch, nb, MOBA_KV_HEADS, HEAD_DIM)
    km = jnp.repeat(km, MOBA_HEADS // MOBA_KV_HEADS, axis=2)
    km = jnp.pad(jnp.transpose(km, (0, 2, 3, 1)), ((0, 0), (0, 0), (0, 0), (0, MOBA_MAX_BLOCKS - nb)))
    km = jnp.einsum("bhdn,hg->bhdgn", km, jnp.eye(MOBA_HEADS, dtype=F32)).reshape(batch, 512, LANES)
    hi = km.astype(BF16)
    lo = (km - hi.astype(F32)).astype(BF16)
    return jnp.stack([hi, lo], axis=1)


def _moba_gate(q, km, tiles_per_seq):
    n = q.shape[0]
    return pl.pallas_call(
        functools.partial(_moba_gate_kernel, tiles_per_seq=tiles_per_seq),
        grid=(n // ROW_TILE,),
        in_specs=[pl.BlockSpec((ROW_TILE, 512), lambda i: (i, 0)),
                  pl.BlockSpec((1, 2, 512, LANES), lambda i: (i // tiles_per_seq, 0, 0, 0))],
        out_specs=pl.BlockSpec((MOBA_HEADS, ROW_TILE, LANES), lambda i: (0, i, 0)),
        out_shape=jax.ShapeDtypeStruct((MOBA_HEADS, n, LANES), BF16),
        compiler_params=pltpu.CompilerParams(dimension_semantics=("parallel",), vmem_limit_bytes=VMEM_LIMIT),
        name="moba_gate",
    )(q, km)


PAGE_CHUNK = 16


class _PagedCaches:
    def __init__(self, pt_ref, caches, bufs, sems, layer, pps):
        self.pt, self.caches, self.bufs, self.sems, self.layer, self.pps = pt_ref, caches, bufs, sems, layer, pps

    def _copy(self, a, page, slot, p):
        return pltpu.make_async_copy(self.caches[a].at[self.layer, page], self.bufs[a].at[slot, p],
                                     self.sems.at[a, slot])

    def start(self, s, j, slot):
        for p in range(self.pps):
            page = self.pt[s, j * self.pps + p]
            for a in range(len(self.caches)):
                self._copy(a, page, slot, p).start()

    def wait(self, slot):
        for a in range(len(self.caches)):
            for p in range(self.pps):
                self._copy(a, 0, slot, p).wait()

    def step(self):
        s, j = pl.program_id(0), pl.program_id(1)
        ns, nj = pl.num_programs(0), pl.num_programs(1)
        t = s * nj + j
        slot = t % 2

        @pl.when(t == 0)
        def _():
            self.start(s, j, slot)

        @pl.when(t + 1 < ns * nj)
        def _():
            wrap = j + 1 == nj
            self.start(jnp.where(wrap, s + 1, s), jnp.where(wrap, 0, j + 1), 1 - slot)

        self.wait(slot)
        return slot

    def pages(self, a, slot, lo, hi, axis):
        parts = [self.bufs[a][slot, p] for p in range(lo, hi)]
        return parts[0] if len(parts) == 1 else jnp.concatenate(parts, axis=axis)


def _paged_scratch(shapes, pps):
    return ([pltpu.VMEM((2, pps) + shp, F32) for shp in shapes]
            + [pltpu.SemaphoreType.DMA((len(shapes), 2))])


def _softmax_step(s, pv, m, l, acc):
    m_new = jnp.maximum(m, jnp.max(s, axis=-1, keepdims=True))
    alpha = jnp.exp(m - m_new)
    p = jnp.exp(s - m_new)
    return m_new, alpha * l + jnp.sum(p, axis=-1, keepdims=True), alpha * acc + pv(p.astype(BF16))


def _new_row_mask(shape, dec_seq):
    qidx = lax.broadcasted_iota(jnp.int32, shape, 0) % dec_seq
    return lax.broadcasted_iota(jnp.int32, shape, 1) <= qidx


def _mla_dec_kernel(pt_ref, ckv_hbm, krt_hbm, q_ref, cnew_ref, krnew_ref, wukg_ref, wukt_ref, wuv_ref, g_ref, o_ref,
                    qp_sc, qr_sc, m_sc, l_sc, acc_sc, ckv_buf, krt_buf, sems, *, layer, pps, dec_seq):
    j = pl.program_id(1)
    rows = MLA_HEADS * dec_seq
    paged = _PagedCaches(pt_ref, (ckv_hbm, krt_hbm), (ckv_buf, krt_buf), sems, layer, pps)
    slot = paged.step()

    @pl.when(j == 0)
    def _():
        for hh in range(MLA_HEADS):
            qh = q_ref[hh]
            qp_sc[dec_seq * hh:dec_seq * (hh + 1), :] = _dot(qh[:, 0:64].astype(BF16), wukg_ref[hh])
            qr_sc[dec_seq * hh:dec_seq * (hh + 1), :] = qh[:, 64:96]
        m_sc[...] = jnp.full_like(m_sc, M_INIT)
        l_sc[...] = jnp.zeros_like(l_sc)
        acc_sc[...] = jnp.zeros_like(acc_sc)

    lhs = jnp.concatenate([qp_sc[...].astype(BF16), wukt_ref[...]], axis=0)
    qr = qr_sc[...].astype(BF16)
    g = g_ref[...]

    def chunk(c, krt, carry, masked):
        keys = c.shape[0]
        cn = (c * lax.rsqrt(jnp.mean(c * c, axis=-1, keepdims=True) + EPS) * g).astype(BF16)
        big = _dot_nt(lhs, cn)
        kexp = big[rows:, :]
        ss = jnp.sum((kexp * kexp).reshape(MLA_HEADS, MLA_NOPE_DIM, keys), axis=1)
        r = lax.rsqrt(ss * (1.0 / MLA_NOPE_DIM) + EPS)
        s = big[0:rows, :].reshape(MLA_HEADS, dec_seq, keys) * r[:, None, :]
        s = s.reshape(rows, keys) + _dot(qr, krt.astype(BF16))
        if masked:
            s = jnp.where(_new_row_mask(s.shape, dec_seq), s, -jnp.inf)
        return _softmax_step(s, lambda p: _dot(p, cn), *carry)

    carry = (m_sc[...], l_sc[...], acc_sc[...])
    for p0 in range(0, pps, PAGE_CHUNK):
        p1 = min(p0 + PAGE_CHUNK, pps)
        carry = chunk(paged.pages(0, slot, p0, p1, 0), paged.pages(1, slot, p0, p1, 1), carry, False)
    m_sc[...], l_sc[...], acc_sc[...] = carry

    @pl.when(j == pl.num_programs(1) - 1)
    def _():
        m, l, acc = chunk(cnew_ref[0], krnew_ref[0], (m_sc[...], l_sc[...], acc_sc[...]), True)
        lat = acc / l
        for hh in range(MLA_HEADS):
            lat_h = lat[dec_seq * hh:dec_seq * (hh + 1), :].astype(BF16)
            o_ref[0, :, 64 * hh:64 * (hh + 1)] = _dot(lat_h, wuv_ref[hh])


def _mla_decode(cache_ckv, cache_kr, layer, page_table, qm, cnew, krnew, wukg, wukt, wuv, g_ckv, pps, dec_seq):
    n_seq, n_pages = page_table.shape
    rows = MLA_HEADS * dec_seq
    per_seq = lambda shp: pl.BlockSpec((1,) + shp, lambda s, j, pt: (s,) + (0,) * len(shp))
    cst = lambda a: pl.BlockSpec(a.shape, lambda s, j, pt: (0,) * a.ndim)
    hbm = pl.BlockSpec(memory_space=pl.ANY)
    grid_spec = pltpu.PrefetchScalarGridSpec(
        num_scalar_prefetch=1, grid=(n_seq, n_pages // pps),
        in_specs=[hbm, hbm, pl.BlockSpec((MLA_HEADS, dec_seq, LANES), lambda s, j, pt: (0, s, 0)),
                  per_seq((LANES, MLA_KV_RANK)), per_seq((MLA_ROPE_DIM, LANES)),
                  cst(wukg), cst(wukt), cst(wuv), cst(g_ckv)],
        out_specs=per_seq((dec_seq, MLA_HEADS * MLA_V_DIM)),
        scratch_shapes=[pltpu.VMEM((rows, MLA_KV_RANK), F32), pltpu.VMEM((rows, MLA_ROPE_DIM), F32),
                        pltpu.VMEM((rows, 1), F32), pltpu.VMEM((rows, 1), F32), pltpu.VMEM((rows, MLA_KV_RANK), F32)]
        + _paged_scratch([(LANES, MLA_KV_RANK), (MLA_ROPE_DIM, LANES)], pps))
    return pl.pallas_call(
        functools.partial(_mla_dec_kernel, layer=layer, pps=pps, dec_seq=dec_seq),
        grid_spec=grid_spec,
        out_shape=jax.ShapeDtypeStruct((n_seq, dec_seq, MLA_HEADS * MLA_V_DIM), F32),
        compiler_params=pltpu.CompilerParams(dimension_semantics=("arbitrary", "arbitrary"),
                                             vmem_limit_bytes=VMEM_LIMIT),
        name="mla_decode",
    )(page_table, cache_ckv, cache_kr, qm, cnew, krnew, wukg, wukt, wuv, g_ckv)


def _fox_dec_kernel(pt_ref, k_hbm, v_hbm, lf_hbm, qbd_ref, knew_ref, vnew_ref, lfnew_ref, utri_ref, o_ref,
                    m_sc, l_sc, acc_sc, base_sc, k_buf, v_buf, lf_buf, sems, *, layer, pps, dec_seq):
    j = pl.program_id(1)
    rows = FOX_HEADS * dec_seq
    paged = _PagedCaches(pt_ref, (k_hbm, v_hbm, lf_hbm), (k_buf, v_buf, lf_buf), sems, layer, pps)
    slot = paged.step()

    @pl.when(j == 0)
    def _():
        m_sc[...] = jnp.full_like(m_sc, M_INIT)
        l_sc[...] = jnp.zeros_like(l_sc)
        acc_sc[...] = jnp.zeros_like(acc_sc)
        base_sc[...] = jnp.zeros_like(base_sc)

    qbd = qbd_ref[0]
    utri = utri_ref[...]

    def cum_lanes(lf):
        a, b, c = _split3(lf)
        return _dot(a, utri) + _dot(b, utri) + _dot(c, utri)

    def chunk(kt, vt, c, carry, masked):
        keys = kt.shape[1]
        s = _dot(qbd, kt.astype(BF16)).reshape(FOX_HEADS, dec_seq, keys) - c[:, None, :]
        s = s.reshape(rows, keys)
        if masked:
            s = jnp.where(_new_row_mask(s.shape, dec_seq), s, -jnp.inf)
        vtb = vt.astype(BF16)
        return _softmax_step(s, lambda p: _dot_nt(p, vtb), *carry)

    cum = cum_lanes(paged.pages(2, slot, 0, pps, 0))
    base = base_sc[...]
    c_pages = []
    for p in range(pps):
        c_loc = cum[FOX_HEADS * p:FOX_HEADS * (p + 1), :]
        c_pages.append(c_loc + base)
        base = base + c_loc[:, LANES - 1:LANES]
    base_sc[...] = base
    carry = (m_sc[...], l_sc[...], acc_sc[...])
    for p0 in range(0, pps, PAGE_CHUNK):
        p1 = min(p0 + PAGE_CHUNK, pps)
        c = c_pages[p0] if p1 - p0 == 1 else jnp.concatenate(c_pages[p0:p1], axis=1)
        carry = chunk(paged.pages(0, slot, p0, p1, 1), paged.pages(1, slot, p0, p1, 1), c, carry, False)
    m_sc[...], l_sc[...], acc_sc[...] = carry

    @pl.when(j == pl.num_programs(1) - 1)
    def _():
        c_new = cum_lanes(lfnew_ref[0]) + base_sc[...]
        m, l, acc = chunk(knew_ref[0], vnew_ref[0], c_new, (m_sc[...], l_sc[...], acc_sc[...]), True)
        o = acc / l
        for hq in range(FOX_HEADS):
            hk = hq // 2
            o_ref[0, :, 64 * hq:64 * (hq + 1)] = o[dec_seq * hq:dec_seq * (hq + 1), 64 * hk:64 * (hk + 1)]


def _fox_decode(cache_k, cache_v, cache_lft, layer, page_table, qbd, knew, vnew, lfnew, pps, dec_seq):
    n_seq, n_pages = page_table.shape
    rows = FOX_HEADS * dec_seq
    per_seq = lambda shp: pl.BlockSpec((1,) + shp, lambda s, j, pt: (s,) + (0,) * len(shp))
    utri = jnp.asarray(np.triu(np.ones((LANES, LANES), np.float32)), BF16)
    hbm = pl.BlockSpec(memory_space=pl.ANY)
    grid_spec = pltpu.PrefetchScalarGridSpec(
        num_scalar_prefetch=1, grid=(n_seq, n_pages // pps),
        in_specs=[hbm, hbm, hbm,
                  per_seq((rows, 256)), per_seq((256, LANES)), per_seq((256, LANES)), per_seq((FOX_HEADS, LANES)),
                  pl.BlockSpec((LANES, LANES), lambda s, j, pt: (0, 0))],
        out_specs=per_seq((dec_seq, FOX_HEADS * HEAD_DIM)),
        scratch_shapes=[pltpu.VMEM((rows, 1), F32), pltpu.VMEM((rows, 1), F32), pltpu.VMEM((rows, 256), F32),
                        pltpu.VMEM((FOX_HEADS, 1), F32)]
        + _paged_scratch([(256, LANES), (256, LANES), (FOX_HEADS, LANES)], pps))
    return pl.pallas_call(
        functools.partial(_fox_dec_kernel, layer=layer, pps=pps, dec_seq=dec_seq),
        grid_spec=grid_spec,
        out_shape=jax.ShapeDtypeStruct((n_seq, dec_seq, FOX_HEADS * HEAD_DIM), F32),
        compiler_params=pltpu.CompilerParams(dimension_semantics=("arbitrary", "arbitrary"),
                                             vmem_limit_bytes=VMEM_LIMIT),
        name="fox_decode",
    )(page_table, cache_k, cache_v, cache_lft, qbd, knew, vnew, lfnew, utri)


def _moba_dec_kernel(pt_ref, k_hbm, v_hbm, qbd_ref, qg_ref, knew_ref, vnew_ref, o_ref,
                     m_sc, l_sc, kmt_sc, o_sc, k_buf, v_buf, sems, *, layer, pps, dec_seq, nb):
    j = pl.program_id(1)
    rows = MOBA_HEADS * dec_seq
    bps = pps // 2
    paged = _PagedCaches(pt_ref, (k_hbm, v_hbm), (k_buf, v_buf), sems, layer, pps)
    slot = paged.step()
    qbd = qbd_ref[0]
    lane = _lane_iota((rows, LANES))
    lane_k = _lane_iota((256, LANES))

    @pl.when(j == 0)
    def _():
        m_sc[...] = jnp.zeros_like(m_sc)
        l_sc[...] = jnp.zeros_like(l_sc)
        kmt_sc[...] = jnp.zeros_like(kmt_sc)

    m_all, l_all, kmt = m_sc[...], l_sc[...], kmt_sc[...]
    for b in range(bps):
        n = j * bps + b
        kt = paged.pages(0, slot, 2 * b, 2 * b + 2, 1)
        vt = paged.pages(1, slot, 2 * b, 2 * b + 2, 1)
        s = _dot(qbd, kt.astype(BF16))
        mn = jnp.max(s, axis=-1, keepdims=True)
        p = jnp.exp(s - mn)
        o_sc[n] = _dot_nt(p.astype(BF16), vt.astype(BF16))
        here = lane == n
        m_all = jnp.where(here, mn, m_all)
        l_all = jnp.where(here, jnp.sum(p, axis=-1, keepdims=True), l_all)
        kmt = jnp.where(lane_k == n, jnp.sum(kt, axis=-1, keepdims=True) * (1.0 / MOBA_BLOCK), kmt)
    m_sc[...], l_sc[...], kmt_sc[...] = m_all, l_all, kmt

    @pl.when(j == pl.num_programs(1) - 1)
    def _():
        s = _dot(qbd, knew_ref[0].astype(BF16))
        s = jnp.where(_new_row_mask(s.shape, dec_seq), s, -jnp.inf)
        m_o = jnp.max(s, axis=-1, keepdims=True)
        p = jnp.exp(s - m_o)
        l_o = jnp.sum(p, axis=-1, keepdims=True)
        o_o = _dot_nt(p.astype(BF16), vnew_ref[0].astype(BF16))
        qg = qg_ref[0]
        qh = qg.astype(BF16)
        ql = (qg - qh.astype(F32)).astype(BF16)
        kh = kmt.astype(BF16)
        kl = (kmt - kh.astype(F32)).astype(BF16)
        g_all = _dot(qh, kh) + _dot(qh, kl) + _dot(ql, kh)
        g = jnp.where(lane < nb, g_all, -jnp.inf)
        sel = jnp.zeros(g.shape, jnp.bool_)
        for _ in range(min(MOBA_TOPK, nb)):
            vmax, idx = _first_index_of_max(g, lane)
            hit = lane == idx
            sel = jnp.logical_or(sel, jnp.logical_and(hit, vmax > -jnp.inf))
            g = jnp.where(hit, -jnp.inf, g)
        m_top = jnp.maximum(jnp.max(jnp.where(sel, m_all, -jnp.inf), axis=-1, keepdims=True), m_o)
        w = jnp.where(sel, jnp.exp(m_all - m_top), 0.0)
        w_o = jnp.exp(m_o - m_top)
        den = jnp.sum(w * l_all, axis=-1, keepdims=True) + w_o * l_o

        def body(nn, acc):
            col = jnp.sum(jnp.where(lane == nn, w, 0.0), axis=-1, keepdims=True)
            return acc + col * o_sc[nn]

        o = lax.fori_loop(0, nb, body, w_o * o_o, unroll=8) / den
        for hq in range(MOBA_HEADS):
            hk = hq // 2
            o_ref[0, :, 64 * hq:64 * (hq + 1)] = o[dec_seq * hq:dec_seq * (hq + 1), 64 * hk:64 * (hk + 1)]


def _moba_decode(cache_k, cache_v, layer, page_table, qbd, qg, knew, vnew, pps, dec_seq):
    n_seq, n_pages = page_table.shape
    rows = MOBA_HEADS * dec_seq
    nb = n_pages // 2
    assert nb <= LANES
    per_seq = lambda shp: pl.BlockSpec((1,) + shp, lambda s, j, pt: (s,) + (0,) * len(shp))
    hbm = pl.BlockSpec(memory_space=pl.ANY)
    grid_spec = pltpu.PrefetchScalarGridSpec(
        num_scalar_prefetch=1, grid=(n_seq, n_pages // pps),
        in_specs=[hbm, hbm, per_seq((rows, 256)), per_seq((rows, 256)), per_seq((256, LANES)), per_seq((256, LANES))],
        out_specs=per_seq((dec_seq, MOBA_HEADS * HEAD_DIM)),
        scratch_shapes=[pltpu.VMEM((rows, LANES), F32), pltpu.VMEM((rows, LANES), F32), pltpu.VMEM((256, LANES), F32),
                        pltpu.VMEM((nb, rows, 256), F32)]
        + _paged_scratch([(256, LANES), (256, LANES)], pps))
    return pl.pallas_call(
        functools.partial(_moba_dec_kernel, layer=layer, pps=pps, dec_seq=dec_seq, nb=nb),
        grid_spec=grid_spec,
        out_shape=jax.ShapeDtypeStruct((n_seq, dec_seq, MOBA_HEADS * HEAD_DIM), F32),
        compiler_params=pltpu.CompilerParams(dimension_semantics=("arbitrary", "arbitrary"),
                                             vmem_limit_bytes=VMEM_LIMIT),
        name="moba_decode",
    )(page_table, cache_k, cache_v, qbd, qg, knew, vnew)


def _block_diag_queries(q4):
    nh = q4.shape[0]
    place = jnp.asarray(np.arange(nh)[:, None] // 2 == np.arange(nh // 2)[None, :], q4.dtype)
    out = jnp.einsum("hsqd,hk->shqkd", q4, place)
    return out.reshape(q4.shape[1], nh * q4.shape[2], (nh // 2) * 64)


def _pad_rows(a, rows):
    return jnp.pad(a, ((0, 0), (0, rows - a.shape[1]), (0, 0)))


def kernel(x_prompt, x_sample, cache_mla_ckv, cache_mla_krope, cache_fox_k, cache_fox_v, cache_fox_logf, cache_moba_k, cache_moba_v, page_table, norm_mix, norm_ffn, w_in_a, mla_g_cq, mla_w_uq, mla_g_ckv, mla_w_ukv, mla_g_qn, mla_g_qr, mla_g_kn, mla_g_kr, fox_g_q, fox_g_k, fox_b_f, w_out_a, w_in_b, moba_g_q, moba_g_k, gm_ln_g, gm_ln_b, gm_w_s, gm_b_s, w_out_b, moe_w_group, moe_b_group, moe_w_sub, moe_b_sub, moe_w1, moe_w3, moe_w2):
    B, T, _ = x_prompt.shape
    S, DS, _ = x_sample.shape
    n_pages = page_table.shape[1]
    page = cache_mla_ckv.shape[2]
    past = n_pages * page
    n_phys = cache_mla_ckv.shape[1]
    assert page == LANES and T % 1024 == 0 and T // MOBA_BLOCK <= MOBA_MAX_BLOCKS
    assert (S * DS) % ROW_TILE == 0 and ROW_TILE % DS == 0 and DS <= LANES
    assert past % MOBA_BLOCK == 0 and past % GM_CHUNK == 0 and n_pages % 2 == 0
    pps = math.gcd(16, n_pages)
    tps = T // ROW_TILE
    np_rows, ns_rows = B * T, S * DS
    xp = x_prompt.reshape(np_rows, D_MODEL)
    xs = x_sample.reshape(ns_rows, D_MODEL)
    pos_s = past + np.arange(ROW_TILE) % DS

    pa = _prep_a(0, norm_mix[0], w_in_a, mla_g_cq, mla_w_uq, mla_g_ckv, mla_w_ukv, mla_g_qn, mla_g_qr, mla_g_kn,
                 mla_g_kr, fox_g_q, fox_g_k, fox_b_f)
    p_ckv, p_kr, p_fk, p_fv, p_fl, qm, km, vm, qf, kf, vf = _proj_a(xp, pa, _rot_tables(jnp.arange(T)), tps)
    s_ckv, s_kr, s_fk, s_fv, s_fl, qm_s, _, _, qf_s, _, _ = _proj_a(xs, pa, _rot_tables(jnp.asarray(pos_s)), 1)
    mix_p = jnp.concatenate([_flash(qm, km, vm, B, T, tile=FLASH_TILE),
                             _flash(qf, kf, vf, B, T, tile=FLASH_TILE)], axis=1)

    wukv = mla_w_ukv[0].reshape(MLA_KV_RANK, MLA_HEADS, 128)
    wuk_t = jnp.transpose(wukv[:, :, 0:64], (1, 2, 0))
    wukg = (wuk_t * mla_g_kn[0][None, :, None]).astype(BF16)
    wukt = wuk_t.reshape(MLA_HEADS * 64, MLA_KV_RANK).astype(BF16)
    wuv = jnp.transpose(wukv[:, :, 64:128], (1, 0, 2)).astype(BF16)
    rows_last = lambda c: jnp.moveaxis(c, 2, -1).reshape(c.shape[0], n_phys, -1, page)
    new_t = lambda a: jnp.pad(jnp.transpose(a.reshape(S, DS, -1), (0, 2, 1)), ((0, 0), (0, 0), (0, LANES - DS)))
    o_mla = _mla_decode(cache_mla_ckv, rows_last(cache_mla_krope), 0, page_table, qm_s.astype(F32),
                        _pad_rows(s_ckv.reshape(S, DS, -1), LANES), new_t(s_kr),
                        wukg, wukt, wuv, mla_g_ckv[0][None, :], pps, DS)
    qbd_f = _block_diag_queries(qf_s[:, :, 0:64].reshape(FOX_HEADS, S, DS, 64))
    o_fox = _fox_decode(rows_last(cache_fox_k), rows_last(cache_fox_v), rows_last(cache_fox_logf),
                        0, page_table, qbd_f, new_t(s_fk), new_t(s_fv), new_t(s_fl), pps, DS)
    mix_s = jnp.concatenate([o_mla.reshape(ns_rows, -1), o_fox.reshape(ns_rows, -1)], axis=1).astype(BF16)

    pm = _prep_moe(0, norm_ffn, moe_w_group, moe_b_group, moe_w_sub, moe_b_sub, moe_w1, moe_w3, moe_w2)
    w_out = w_out_a[0].astype(BF16)
    xp = _out_moe(xp, mix_p, w_out, pm, min(1024, np_rows))
    xs = _out_moe(xs, mix_s, w_out, pm, min(1024, ns_rows))

    pb = _prep_b(0, norm_mix[1], w_in_b, moba_g_q, moba_g_k, gm_ln_g, gm_ln_b)
    wmix_p, bmix_p = _gmlp_mix_weights(gm_w_s[0], gm_b_s[0], np.arange(ROW_TILE), GM_CHUNK)
    wmix_s, bmix_s = _gmlp_mix_weights(gm_w_s[0], gm_b_s[0], pos_s, DS)
    q_p, p_mk, p_mv, _, gm_p, kmean, ka, va = _proj_b(xp, pb, wmix_p.astype(BF16), bmix_p, tps)
    q_s, s_mk, s_mv, s_gv, gm_s, _, _, _ = _proj_b(xs, pb, wmix_s.astype(BF16), bmix_s, 1)
    qa = _moba_gate(q_p, _moba_km(kmean, B, tps), tps)
    mix_p = jnp.concatenate([_flash(qa, ka, va, B, T, tile=FLASH_TILE), gm_p], axis=1)

    q4 = jnp.transpose(q_s.reshape(S, DS, MOBA_HEADS, 64), (2, 0, 1, 3))
    o_moba = _moba_decode(rows_last(cache_moba_k), rows_last(cache_moba_v), 0,
                          page_table, _block_diag_queries((q4 * ATTN_SCALE).astype(BF16)), _block_diag_queries(q4),
                          new_t(s_mk), new_t(s_mv), pps, DS)
    mix_s = jnp.concatenate([o_moba.reshape(ns_rows, -1).astype(BF16), gm_s], axis=1)

    pm = _prep_moe(1, norm_ffn, moe_w_group, moe_b_group, moe_w_sub, moe_b_sub, moe_w1, moe_w3, moe_w2)
    w_out = w_out_b[0].astype(BF16)
    xp = _out_moe(xp, mix_p, w_out, pm, min(1024, np_rows))
    xs = _out_moe(xs, mix_s, w_out, pm, min(1024, ns_rows))

    pr = lambda a, *tail: a.reshape((1, B, T) + tail)
    sr = lambda a, *tail: a.reshape((1, S, DS) + tail)
    return (xp.reshape(B, T, D_MODEL), xs.reshape(S, DS, D_MODEL),
            pr(p_ckv, MLA_KV_RANK), pr(p_kr, MLA_ROPE_DIM), pr(p_fk, FOX_KV_HEADS, 64), pr(p_fv, FOX_KV_HEADS, 64),
            pr(p_fl, FOX_HEADS), pr(p_mk, MOBA_KV_HEADS, 64), pr(p_mv, MOBA_KV_HEADS, 64),
            sr(s_ckv, MLA_KV_RANK), sr(s_kr, MLA_ROPE_DIM), sr(s_fk, FOX_KV_HEADS, 64), sr(s_fv, FOX_KV_HEADS, 64),
            sr(s_fl, FOX_HEADS), sr(s_mk, MOBA_KV_HEADS, 64), sr(s_mv, MOBA_KV_HEADS, 64), sr(s_gv, GM_WIDTH))
```
